```python
import math
import jax
import jax.numpy as jnp
from jax import lax
import numpy as np

D_MODEL = 1024
BATCH = 8
SEQ = 4096
DEPTH = 4

N_META = 16
CHUNK = 64
N_EVEN = (DEPTH + 1) // 2
N_ODD = DEPTH // 2
RMS_EPS = 1e-6

S5_WIDTH = 256
S5_GROUP = 16
S5_GROUPS = S5_WIDTH // S5_GROUP
S5_STATE = 64

SSD_HEADDIM = 64
SSD_INNER = 768
SSD_HEADS = SSD_INNER // SSD_HEADDIM
SSD_GROUPS = 2
SSD_STATE = 128
SSD_CONV = 4
SSD_XBC = SSD_INNER + 2 * SSD_GROUPS * SSD_STATE
EVEN_IN = S5_WIDTH + SSD_INNER + SSD_XBC + SSD_HEADS
MIX_WIDTH = S5_WIDTH + SSD_INNER

RWKV_WIDTH = 512
RWKV_HEADDIM = 64
RWKV_HEADS = RWKV_WIDTH // RWKV_HEADDIM
LORA_W = 64
LORA_A = 64
LORA_V = 32
LORA_G = 128
GN_EPS = 64e-5
RWKV_IN = 3 * RWKV_WIDTH + LORA_W + LORA_A + LORA_G

HGRN_WIDTH = 512
HGRN_HEADS = 4
HGRN_HEADDIM = HGRN_WIDTH // HGRN_HEADS
HGRN_IN = 4 * HGRN_WIDTH
ODD_IN = RWKV_IN + HGRN_IN

D_FF = 2816
FFN_CONV = 3

kernel_name = "hybrid_s5_ssd_rwkv7_hgrn2_convffn"


def rms_norm(x, g, eps=RMS_EPS):
    xf = x.astype(jnp.float32)
    y = xf * lax.rsqrt(jnp.mean(xf * xf, axis=-1, keepdims=True) + eps)
    return (y * g.astype(jnp.float32)).astype(x.dtype)


def causal_dwconv(x, w, b):
    width, ch = w.shape
    y = lax.conv_general_dilated(
        x, w[:, None, :].astype(x.dtype), window_strides=(1,), padding=[(width - 1, 0)],
        dimension_numbers=("NWC", "WIO", "NWC"), feature_group_count=ch)
    return y + b.astype(x.dtype)


def front_pad(t, n):
    return jnp.pad(t, [(0, 0), (n, 0)] + [(0, 0)] * (t.ndim - 2))


def token_shift(p, mu):
    prev = jnp.pad(p, ((0, 0), (1, 0), (0, 0)))[:, :-1]
    return p + (prev - p) * mu.astype(p.dtype)


def _complex_affine_combine(e1, e2):
    a1r, a1i, b1r, b1i = e1
    a2r, a2i, b2r, b2i = e2
    return (a2r * a1r - a2i * a1i,
            a2r * a1i + a2i * a1r,
            a2r * b1r - a2i * b1i + b2r,
            a2r * b1i + a2i * b1r + b2i)


def s5_mixer(u, lam_re, lam_im, log_dt, b_re, b_im, c_re, c_im, d_skip, w_glu, b_glu):
    f32 = jnp.float32
    bsz, length, _ = u.shape
    ug = u.astype(f32).reshape(bsz, length, S5_GROUPS, S5_GROUP)
    lr, li = lam_re.astype(f32), lam_im.astype(f32)
    dt = jnp.exp(log_dt.astype(f32))[:, None]
    mag = jnp.exp(lr * dt)
    ab_re, ab_im = mag * jnp.cos(li * dt), mag * jnp.sin(li * dt)
    den = lr * lr + li * li
    zr, zi = ab_re - 1.0, ab_im
    f_re = (zr * lr + zi * li) / den
    f_im = (zi * lr - zr * li) / den
    br, bi = b_re.astype(f32), b_im.astype(f32)
    bb_re = f_re[..., None] * br - f_im[..., None] * bi
    bb_im = f_re[..., None] * bi + f_im[..., None] * br
    bu_re = jnp.einsum("blgc,gpc->blgp", ug, bb_re)
    bu_im = jnp.einsum("blgc,gpc->blgp", ug, bb_im)
    a_re = jnp.broadcast_to(ab_re, (1, length) + ab_re.shape)
    a_im = jnp.broadcast_to(ab_im, (1, length) + ab_im.shape)
    _, _, h_re, h_im = lax.associative_scan(
        _complex_affine_combine, (a_re, a_im, bu_re, bu_im), axis=1)
    y = (jnp.einsum("blgp,gcp->blgc", h_re, c_re.astype(f32))
         - jnp.einsum("blgp,gcp->blgc", h_im, c_im.astype(f32))
         + d_skip.astype(f32).reshape(S5_GROUPS, S5_GROUP) * ug)
    y = jax.nn.gelu(y.reshape(bsz, length, S5_WIDTH))
    out = y * jax.nn.sigmoid(y @ w_glu.astype(f32) + b_glu.astype(f32))
    return out.astype(u.dtype)


def ssd_mixer(z, xbc, dt_raw, conv_w, conv_b, dt_bias, a_log, d_skip, norm_w):
    f32 = jnp.float32
    bsz, length, _ = z.shape
    pad = CHUNK - N_META
    n_chunks = (length + pad) // CHUNK
    hpg = SSD_HEADS // SSD_GROUPS
    xbc = jax.nn.silu(causal_dwconv(xbc, conv_w, conv_b)).astype(f32)
    dt = jax.nn.softplus(dt_raw.astype(f32) + dt_bias.astype(f32))
    xbc, dt = front_pad(xbc, pad), front_pad(dt, pad)
    gn = SSD_GROUPS * SSD_STATE
    xs = xbc[..., :SSD_INNER].reshape(bsz, n_chunks, CHUNK, SSD_GROUPS, hpg, SSD_HEADDIM)
    bmat = xbc[..., SSD_INNER:SSD_INNER + gn].reshape(bsz, n_chunks, CHUNK, SSD_GROUPS, SSD_STATE)
    cmat = xbc[..., SSD_INNER + gn:].reshape(bsz, n_chunks, CHUNK, SSD_GROUPS, SSD_STATE)
    dt = dt.reshape(bsz, n_chunks, CHUNK, SSD_GROUPS, hpg)
    a = -jnp.exp(a_log.astype(f32)).reshape(SSD_GROUPS, hpg)
    xdt = xs * dt[..., None]
    acum = jnp.cumsum(dt * a, axis=2)
    causal = jnp.tril(jnp.ones((CHUNK, CHUNK), dtype=bool))
    seg = acum[:, :, :, None] - acum[:, :, None, :]
    seg = jnp.exp(jnp.where(causal[:, :, None, None], seg, -jnp.inf))
    cb = jnp.einsum("bclgn,bcsgn->bclsg", cmat, bmat)
    y_diag = jnp.einsum("bclsg,bclsgj,bcsgjp->bclgjp", cb, seg, xdt)
    decay_to_end = jnp.exp(acum[:, :, -1:] - acum)
    chunk_states = jnp.einsum("bclgn,bclgj,bclgjp->bcgjpn", bmat, decay_to_end, xdt)
    chunk_decay = jnp.exp(acum[:, :, -1])

    def step(state, inp):
        st, dec = inp
        return state * dec[..., None, None] + st, state

    init = jnp.zeros((bsz, SSD_GROUPS, hpg, SSD_HEADDIM, SSD_STATE), f32)
    _, prev = lax.scan(step, init, (jnp.moveaxis(chunk_states, 1, 0), jnp.moveaxis(chunk_decay, 1, 0)))
    prev = jnp.moveaxis(prev, 0, 1)
    y_off = jnp.einsum("bclgn,bcgjpn,bclgj->bclgjp", cmat, prev, jnp.exp(acum))
    y = y_diag + y_off + xs * d_skip.astype(f32).reshape(SSD_GROUPS, hpg)[:, :, None]
    y = y.reshape(bsz, n_chunks * CHUNK, SSD_INNER)[:, pad:]
    y = y * jax.nn.silu(z.astype(f32))
    y = rms_norm(y.reshape(bsz, length, SSD_GROUPS, SSD_INNER // SSD_GROUPS),
                 norm_w.reshape(SSD_GROUPS, SSD_INNER // SSD_GROUPS))
    return y.reshape(bsz, length, SSD_INNER).astype(z.dtype)


def rwkv7_mixer(p, w0, w2, a0, a2, g2, k_k, k_a, r_k, ln_w, ln_b, v_first, v0=None, v2=None):
    f32 = jnp.float32
    p = p.astype(f32)
    bsz, length, _ = p.shape
    nh, hd, wd = RWKV_HEADS, RWKV_HEADDIM, RWKV_WIDTH
    r, k, v = p[..., :wd], p[..., wd:2 * wd], p[..., 2 * wd:3 * wd]
    o0 = 3 * wd
    pw = p[..., o0:o0 + LORA_W]
    pa = p[..., o0 + LORA_W:o0 + LORA_W + LORA_A]
    pg = p[..., o0 + LORA_W + LORA_A:RWKV_IN]
    w_log = -jax.nn.softplus(-(w0 + jnp.tanh(pw) @ w2)) - 0.5
    decay = jnp.exp(-jnp.exp(w_log))
    a = jax.nn.sigmoid(a0 + pa @ a2)
    if v_first is None:
        v_first = v
    else:
        pv = p[..., RWKV_IN:]
        v = v + (v_first - v) * jax.nn.sigmoid(v0 + pv @ v2)
    g = jax.nn.sigmoid(pg) @ g2

    def heads(t):
        return t.reshape(bsz, length, nh, hd)

    kk = heads(k * k_k)
    kk = kk * lax.rsqrt(jnp.maximum(jnp.sum(kk * kk, axis=-1, keepdims=True), 1e-24))
    k = k * (1.0 + (a - 1.0) * k_a)
    rh, wh, kh, vh, ah = heads(r), heads(decay), heads(k), heads(v), heads(a)

    def step(state, inp):
        r_t, w_t, k_t, v_t, kk_t, a_t = inp
        sa = jnp.einsum("bhvk,bhk->bhv", state, kk_t)
        state = (state * w_t[:, :, None, :]
                 - sa[..., None] * (kk_t * a_t)[:, :, None, :]
                 + v_t[..., None] * k_t[:, :, None, :])
        return state, jnp.einsum("bhvk,bhk->bhv", state, r_t)

    init = jnp.zeros((bsz, nh, hd, hd), f32)
    xs = tuple(jnp.moveaxis(t, 1, 0) for t in (rh, wh, kh, vh, kk, ah))
    _, o = lax.scan(step, init, xs)
    o = jnp.moveaxis(o, 0, 1)
    mean = jnp.mean(o, axis=-1, keepdims=True)
    var = jnp.mean(jnp.square(o - mean), axis=-1, keepdims=True)
    o = (o - mean) * lax.rsqrt(var + GN_EPS) * ln_w.reshape(nh, hd) + ln_b.reshape(nh, hd)
    o = o + jnp.sum(rh * kh * r_k.reshape(nh, hd), axis=-1, keepdims=True) * vh
    return o.reshape(bsz, length, wd) * g, v_first


def hgrn2_mixer(p, lb, norm_w):
    f32 = jnp.float32
    bsz, length, _ = p.shape
    nh, hd = HGRN_HEADS, HGRN_HEADDIM
    q, f, i, og = jnp.split(p.astype(f32), 4, axis=-1)
    q = jax.nn.silu(q)
    forget = lb + (1.0 - lb) * jax.nn.sigmoid(f)
    log_f = jnp.log(forget)
    k = 1.0 - forget
    pad = CHUNK - N_META
    n_chunks = (length + pad) // CHUNK

    def chunked(t):
        t = front_pad(t, pad).reshape(bsz, n_chunks, CHUNK, nh, hd)
        return t.transpose(1, 0, 3, 2, 4)

    causal = jnp.tril(jnp.ones((CHUNK, CHUNK), dtype=bool))

    def step(state, inp):
        qc, kc, vc, gc = inp
        gcum = jnp.cumsum(gc, axis=2)
        rel = gcum[:, :, :, None, :] - gcum[:, :, None, :, :]
        rel = jnp.exp(jnp.where(causal[:, :, None], rel, -jnp.inf))
        att = jnp.einsum("bhlk,bhsk,bhlsk->bhls", qc, kc, rel)
        out = att @ vc + jnp.einsum("bhlk,bhkv->bhlv", qc * jnp.exp(gcum), state)
        g_end = gcum[:, :, -1:]
        state = (jnp.exp(g_end[:, :, 0])[..., None] * state
                 + jnp.einsum("bhsk,bhsv->bhkv", kc * jnp.exp(g_end - gcum), vc))
        return state, out

    init = jnp.zeros((bsz, nh, hd, hd), f32)
    _, o = lax.scan(step, init, (chunked(q), chunked(k), chunked(i), chunked(log_f)))
    o = o.transpose(1, 0, 3, 2, 4).reshape(bsz, n_chunks * CHUNK, nh, hd)[:, pad:]
    o = rms_norm(o, norm_w.reshape(nh, hd))
    return o.reshape(bsz, length, HGRN_WIDTH) * jax.nn.silu(og)


def conv_ffn(x, w_up, conv_w, conv_b, w_down):
    h = causal_dwconv(x @ w_up, conv_w, conv_b)
    gate, val = jnp.split(h, 2, axis=-1)
    return (jax.nn.gelu(gate, approximate=True) * val) @ w_down


def setup_inputs(seed: int = 0) -> dict:
    key = jax.random.key(seed)
    keys = iter(jax.random.split(key, 96))
    f32 = jnp.float32

    def nrm(shape, scale):
        return jax.random.normal(next(keys), shape, f32) * scale

    def unif(shape, lo, hi):
        return jax.random.uniform(next(keys), shape, f32, lo, hi)

    def gain(shape):
        return 1.0 + nrm(shape, 0.02)

    dt0 = jnp.exp(unif((N_EVEN, SSD_HEADS), math.log(1e-3), math.log(1e-1)))
    ramp = (jnp.arange(RWKV_WIDTH, dtype=f32) / (RWKV_WIDTH - 1)) ** 0.85
    return {
        "x": nrm((BATCH, SEQ, D_MODEL), 1.0),
        "meta": nrm((N_META, D_MODEL), 1.0),
        "norm_mix_pre": gain((DEPTH, D_MODEL)),
        "norm_mix_post": gain((DEPTH, D_MODEL)),
        "norm_ffn_pre": gain((DEPTH, D_MODEL)),
        "norm_ffn_post": gain((DEPTH, D_MODEL)),
        "mix_w_out": nrm((DEPTH, MIX_WIDTH, D_MODEL), MIX_WIDTH ** -0.5),
        "ffn_w_up": nrm((DEPTH, D_MODEL, 2 * D_FF), D_MODEL ** -0.5),
        "ffn_conv_w": nrm((DEPTH, FFN_CONV, 2 * D_FF), FFN_CONV ** -0.5),
        "ffn_conv_b": nrm((DEPTH, 2 * D_FF), 0.02),
        "ffn_w_down": nrm((DEPTH, D_FF, D_MODEL), D_FF ** -0.5),
        "ev_w_in": nrm((N_EVEN, D_MODEL, EVEN_IN), D_MODEL ** -0.5),
        "s5_lam_re": -0.5 + nrm((N_EVEN, S5_GROUPS, S5_STATE), 0.01),
        "s5_lam_im": math.pi * jnp.arange(S5_STATE, dtype=f32) + nrm((N_EVEN, S5_GROUPS, S5_STATE), 0.01),
        "s5_log_dt": unif((N_EVEN, S5_GROUPS), math.log(1e-3), math.log(1e-1)),
        "s5_b_re": nrm((N_EVEN, S5_GROUPS, S5_STATE, S5_GROUP), (2 * S5_GROUP) ** -0.5),
        "s5_b_im": nrm((N_EVEN, S5_GROUPS, S5_STATE, S5_GROUP), (2 * S5_GROUP) ** -0.5),
        "s5_c_re": nrm((N_EVEN, S5_GROUPS, S5_GROUP, S5_STATE), (2 * S5_STATE) ** -0.5),
        "s5_c_im": nrm((N_EVEN, S5_GROUPS, S5_GROUP, S5_STATE), (2 * S5_STATE) ** -0.5),
        "s5_d": nrm((N_EVEN, S5_WIDTH), 1.0),
        "s5_w_glu": nrm((N_EVEN, S5_WIDTH, S5_WIDTH), S5_WIDTH ** -0.5),
        "s5_b_glu": nrm((N_EVEN, S5_WIDTH), 0.02),
        "ssd_conv_w": nrm((N_EVEN, SSD_CONV, SSD_XBC), SSD_CONV ** -0.5),
        "ssd_conv_b": nrm((N_EVEN, SSD_XBC), 0.02),
        "ssd_dt_bias": dt0 + jnp.log(-jnp.expm1(-dt0)),
        "ssd_a_log": jnp.log(unif((N_EVEN, SSD_HEADS), 1.0, 16.0)),
        "ssd_d": 1.0 + nrm((N_EVEN, SSD_HEADS), 0.1),
        "ssd_norm": gain((N_EVEN, SSD_INNER)),
        "od_w_in": nrm((N_ODD, D_MODEL, ODD_IN), D_MODEL ** -0.5),
        "rw_mu": unif((N_ODD, RWKV_IN), 0.0, 1.0),
        "rw_w0": ramp * 5.0 - 6.5 + nrm((N_ODD, RWKV_WIDTH), 0.1),
        "rw_w2": nrm((N_ODD, LORA_W, RWKV_WIDTH), 0.1),
        "rw_a0": nrm((N_ODD, RWKV_WIDTH), 0.1),
        "rw_a2": nrm((N_ODD, LORA_A, RWKV_WIDTH), 0.1),
        "rw_g2": nrm((N_ODD, LORA_G, RWKV_WIDTH), LORA_G ** -0.5),
        "rw_k_k": 0.85 + nrm((N_ODD, RWKV_WIDTH), 0.02),
        "rw_k_a": 1.0 + nrm((N_ODD, RWKV_WIDTH), 0.02),
        "rw_r_k": -0.04 + nrm((N_ODD, RWKV_WIDTH), 0.1),
        "rw_ln_w": gain((N_ODD, RWKV_WIDTH)),
        "rw_ln_b": nrm((N_ODD, RWKV_WIDTH), 0.02),
        "rw_w_vin": nrm((N_ODD - 1, D_MODEL, LORA_V), D_MODEL ** -0.5),
        "rw_mu_v": unif((N_ODD - 1, LORA_V), 0.0, 1.0),
        "rw_v0": 1.0 + nrm((N_ODD - 1, RWKV_WIDTH), 0.1),
        "rw_v2": nrm((N_ODD - 1, LORA_V, RWKV_WIDTH), 0.1),
        "hg_lb_raw": nrm((N_ODD, HGRN_WIDTH), 0.1),
        "hg_norm": gain((N_ODD, HGRN_WIDTH)),
    }


def reference(x, meta, norm_mix_pre, norm_mix_post, norm_ffn_pre, norm_ffn_post, mix_w_out,
              ffn_w_up, ffn_conv_w, ffn_conv_b, ffn_w_down, ev_w_in,
              s5_lam_re, s5_lam_im, s5_log_dt, s5_b_re, s5_b_im, s5_c_re, s5_c_im, s5_d, s5_w_glu, s5_b_glu,
              ssd_conv_w, ssd_conv_b, ssd_dt_bias, ssd_a_log, ssd_d, ssd_norm,
              od_w_in, rw_mu, rw_w0, rw_w2, rw_a0, rw_a2, rw_g2, rw_k_k, rw_k_a, rw_r_k, rw_ln_w, rw_ln_b,
              rw_w_vin, rw_mu_v, rw_v0, rw_v2, hg_lb_raw, hg_norm):
    bsz = x.shape[0]
    h = jnp.concatenate(
        [jnp.broadcast_to(meta.astype(x.dtype)[None], (bsz, N_META, D_MODEL)), x], axis=1)
    lb_w = jax.nn.softmax(hg_lb_raw.astype(jnp.float32), axis=0)
    lb_table = jnp.cumsum(lb_w, axis=0) - lb_w[0]
    v_first = None
    s1 = S5_WIDTH
    s2 = s1 + SSD_INNER
    s3 = s2 + SSD_XBC
    for layer in range(DEPTH):
        hn = rms_norm(h, norm_mix_pre[layer])
        if layer % 2 == 0:
            e = layer // 2
            p = hn @ ev_w_in[e]
            y_a = s5_mixer(p[..., :s1], s5_lam_re[e], s5_lam_im[e], s5_log_dt[e], s5_b_re[e], s5_b_im[e],
                           s5_c_re[e], s5_c_im[e], s5_d[e], s5_w_glu[e], s5_b_glu[e])
            y_b = ssd_mixer(p[..., s1:s2], p[..., s2:s3], p[..., s3:], ssd_conv_w[e], ssd_conv_b[e],
                            ssd_dt_bias[e], ssd_a_log[e], ssd_d[e], ssd_norm[e])
            y = jnp.concatenate([y_a.astype(h.dtype), y_b.astype(h.dtype)], axis=-1)
        else:
            o = layer // 2
            if o == 0:
                p = hn @ od_w_in[o]
                p_rw = token_shift(p[..., :RWKV_IN], rw_mu[o])
                y_c, v_first = rwkv7_mixer(p_rw, rw_w0[o], rw_w2[o], rw_a0[o], rw_a2[o], rw_g2[o], rw_k_k[o],
                                           rw_k_a[o], rw_r_k[o], rw_ln_w[o], rw_ln_b[o], None)
            else:
                w_in = jnp.concatenate([od_w_in[o], rw_w_vin[o - 1]], axis=1)
                p = hn @ w_in
                p_rw = token_shift(jnp.concatenate([p[..., :RWKV_IN], p[..., ODD_IN:]], axis=-1),
                                   jnp.concatenate([rw_mu[o], rw_mu_v[o - 1]]))
                y_c, v_first = rwkv7_mixer(p_rw, rw_w0[o], rw_w2[o], rw_a0[o], rw_a2[o], rw_g2[o], rw_k_k[o],
                                           rw_k_a[o], rw_r_k[o], rw_ln_w[o], rw_ln_b[o], v_first,
                                           rw_v0[o - 1], rw_v2[o - 1])
            y_d = hgrn2_mixer(p[..., RWKV_IN:ODD_IN], lb_table[o], hg_norm[o])
            y = jnp.concatenate([y_c.astype(h.dtype), y_d.astype(h.dtype)], axis=-1)
        h = h + rms_norm(y @ mix_w_out[layer], norm_mix_post[layer])
        hn = rms_norm(h, norm_ffn_pre[layer])
        h = h + rms_norm(conv_ffn(hn, ffn_w_up[layer], ffn_conv_w[layer], ffn_conv_b[layer], ffn_w_down[layer]),
                         norm_ffn_post[layer])
    return h[:, N_META:]
```

```python
import functools

import numpy as np
import jax
import jax.numpy as jnp
from jax import lax
from jax.experimental import pallas as pl
from jax.experimental.pallas import tpu as pltpu

F32 = jnp.float32
BF16 = jnp.bfloat16

D_MODEL = 1024
N_META = 16
CHUNK = 64
PAD = CHUNK - N_META
RMS_EPS = 1e-6

S5_WIDTH = 256
S5_GROUP = 16
S5_GROUPS = S5_WIDTH // S5_GROUP
S5_STATE = 64
S5_NS = S5_GROUPS * S5_STATE

SSD_HEADDIM = 64
SSD_INNER = 768
SSD_HEADS = SSD_INNER // SSD_HEADDIM
SSD_GROUPS = 2
SSD_HPG = SSD_HEADS // SSD_GROUPS
SSD_STATE = 128
SSD_CONV = 4
SSD_XBC = SSD_INNER + 2 * SSD_GROUPS * SSD_STATE
MIX_WIDTH = S5_WIDTH + SSD_INNER

RWKV_WIDTH = 512
RWKV_HEADDIM = 64
RWKV_HEADS = RWKV_WIDTH // RWKV_HEADDIM
LORA_W = 64
LORA_A = 64
LORA_V = 32
LORA_G = 128
GN_EPS = 64e-5
RWKV_IN = 3 * RWKV_WIDTH + LORA_W + LORA_A + LORA_G
RWKV_INP = RWKV_IN + 128

HGRN_WIDTH = 512
HGRN_HEADS = 4
HGRN_HEADDIM = HGRN_WIDTH // HGRN_HEADS
HGRN_IN = 4 * HGRN_WIDTH

D_FF = 2816
FFN_CONV = 3

LANES = 128
SUBLANES = 8
VMEM_LIMIT = 56 * 1024 * 1024


def _cparams(*sem):
    return pltpu.CompilerParams(dimension_semantics=sem, vmem_limit_bytes=VMEM_LIMIT)


def _row_tile(lp, target):
    best = SUBLANES
    for t in range(SUBLANES, min(lp, target) + 1, SUBLANES):
        if lp % t == 0:
            best = t
    return best


def _const_spec(shape):
    nd = len(shape)
    return pl.BlockSpec(shape, lambda *_: (0,) * nd)


def _dot(a, b):
    return jnp.dot(a.astype(BF16), b.astype(BF16), preferred_element_type=F32)


def _dot_nt(a, b):
    return lax.dot_general(a.astype(BF16), b.astype(BF16), (((1,), (1,)), ((), ())),
                           preferred_element_type=F32)


def _dot_tn(a, b):
    return lax.dot_general(a.astype(BF16), b.astype(BF16), (((0,), (0,)), ((), ())),
                           preferred_element_type=F32)


def _split3(x):
    hi = x.astype(BF16)
    r1 = x - hi.astype(F32)
    mid = r1.astype(BF16)
    lo = (r1 - mid.astype(F32)).astype(BF16)
    return hi, mid, lo


def _dot_sel_lhs(sel, x):
    hi, mid, lo = _split3(x)
    d = lambda p: jnp.dot(sel, p, preferred_element_type=F32)
    return d(hi) + d(mid) + d(lo)


def _dot_sel_rhs(x, sel):
    hi, mid, lo = _split3(x)
    d = lambda p: jnp.dot(p, sel, preferred_element_type=F32)
    return d(hi) + d(mid) + d(lo)


def _sigmoid(x):
    return 1.0 / (1.0 + jnp.exp(-x))


def _silu(x):
    return x * _sigmoid(x)


def _softplus(x):
    return jnp.maximum(x, 0.0) + jnp.log(1.0 + jnp.exp(-jnp.abs(x)))


def _gelu_tanh(x):
    return 0.5 * x * (1.0 + jnp.tanh(0.7978845608028654 * (x + 0.044715 * (x * x * x))))


def _rms(x, g):
    return x * lax.rsqrt(jnp.mean(x * x, axis=-1, keepdims=True) + RMS_EPS) * g


def _keep_rows(tm, tiles_per_batch):
    row = lax.broadcasted_iota(jnp.int32, (tm, 1), 0)
    first = (pl.program_id(0) % tiles_per_batch) == 0
    return jnp.logical_or(row >= PAD, jnp.logical_not(first))


def _norm_proj_kernel(x_ref, g_ref, w_ref, *o_refs, splits, chunk):
    hn = _rms(x_ref[...], g_ref[...]).astype(BF16)
    for o_ref, (start, width) in zip(o_refs, splits):
        for c0 in range(0, width, chunk):
            cw = min(chunk, width - c0)
            o_ref[:, c0:c0 + cw] = jnp.dot(hn, w_ref[:, start + c0:start + c0 + cw],
                                           preferred_element_type=F32)


def _norm_proj(h, g, w, splits, tm):
    rows = h.shape[0]
    n = w.shape[1]
    return pl.pallas_call(
        functools.partial(_norm_proj_kernel, splits=splits, chunk=512),
        grid=(rows // tm,),
        in_specs=[pl.BlockSpec((tm, D_MODEL), lambda i: (i, 0)),
                  _const_spec((1, D_MODEL)),
                  _const_spec((D_MODEL, n))],
        out_specs=[pl.BlockSpec((tm, wd), lambda i: (i, 0)) for _, wd in splits],
        out_shape=[jax.ShapeDtypeStruct((rows, wd), F32) for _, wd in splits],
        compiler_params=_cparams("arbitrary"),
        name="norm_proj",
    )(h, g.reshape(1, D_MODEL), w)


def _out_proj_kernel(ya_ref, yb_ref, wa_ref, wb_ref, g_ref, h_ref, o_ref, *, tm, tiles_per_batch):
    o = _dot(ya_ref[...], wa_ref[...]) + _dot(yb_ref[...], wb_ref[...])
    upd = _rms(o, g_ref[...])
    o_ref[...] = h_ref[...] + jnp.where(_keep_rows(tm, tiles_per_batch), upd, 0.0)


def _out_proj(ya, yb, wa, wb, g, h, tm, lp):
    rows = h.shape[0]
    ka, kb = ya.shape[1], yb.shape[1]
    return pl.pallas_call(
        functools.partial(_out_proj_kernel, tm=tm, tiles_per_batch=lp // tm),
        grid=(rows // tm,),
        in_specs=[pl.BlockSpec((tm, ka), lambda i: (i, 0)),
                  pl.BlockSpec((tm, kb), lambda i: (i, 0)),
                  _const_spec((ka, D_MODEL)),
                  _const_spec((kb, D_MODEL)),
                  _const_spec((1, D_MODEL)),
                  pl.BlockSpec((tm, D_MODEL), lambda i: (i, 0))],
        out_specs=pl.BlockSpec((tm, D_MODEL), lambda i: (i, 0)),
        out_shape=jax.ShapeDtypeStruct((rows, D_MODEL), F32),
        compiler_params=_cparams("arbitrary"),
        name="out_proj",
    )(ya, yb, wa, wb, g.reshape(1, D_MODEL), h)


def _ffn_kernel(h_ref, g1_ref, wup_ref, cw_ref, cb_ref, wdn_ref, g2_ref, o_ref, carry_ref, ext_ref,
                *, tm, tf, tiles_per_batch):
    i = pl.program_id(0)

    @pl.when(i % tiles_per_batch == 0)
    def _():
        carry_ref[...] = jnp.zeros_like(carry_ref)

    x = h_ref[...]
    hn = _rms(x, g1_ref[...]).astype(BF16)
    acc = jnp.zeros((tm, D_MODEL), F32)
    for c in range(D_FF // tf):
        halves = []
        for part in range(2):
            col = part * D_FF + c * tf
            u = jnp.dot(hn, wup_ref[:, col:col + tf], preferred_element_type=F32)
            ext_ref[0:SUBLANES, :] = carry_ref[:, col:col + tf]
            ext_ref[SUBLANES:SUBLANES + tm, :] = u
            carry_ref[:, col:col + tf] = u[tm - SUBLANES:tm, :]
            cw = cw_ref[:, col:col + tf]
            halves.append(cw[0:1] * ext_ref[SUBLANES - 2:SUBLANES - 2 + tm, :]
                          + cw[1:2] * ext_ref[SUBLANES - 1:SUBLANES - 1 + tm, :]
                          + cw[2:3] * u + cb_ref[:, col:col + tf])
        act = _gelu_tanh(halves[0]) * halves[1]
        acc = acc + _dot(act, wdn_ref[c * tf:(c + 1) * tf, :])
    upd = _rms(acc, g2_ref[...])
    o_ref[...] = x + jnp.where(_keep_rows(tm, tiles_per_batch), upd, 0.0)


def _ffn(h, g1, wup, cw, cb, wdn, g2, tm, lp):
    rows = h.shape[0]
    tf = 256
    cw8 = jnp.zeros((SUBLANES, 2 * D_FF), F32).at[:FFN_CONV].set(cw)
    return pl.pallas_call(
        functools.partial(_ffn_kernel, tm=tm, tf=tf, tiles_per_batch=lp // tm),
        grid=(rows // tm,),
        in_specs=[pl.BlockSpec((tm, D_MODEL), lambda i: (i, 0)),
                  _const_spec((1, D_MODEL)),
                  _const_spec((D_MODEL, 2 * D_FF)),
                  _const_spec((SUBLANES, 2 * D_FF)),
                  _const_spec((1, 2 * D_FF)),
                  _const_spec((D_FF, D_MODEL)),
                  _const_spec((1, D_MODEL))],
        out_specs=pl.BlockSpec((tm, D_MODEL), lambda i: (i, 0)),
        out_shape=jax.ShapeDtypeStruct((rows, D_MODEL), F32),
        scratch_shapes=[pltpu.VMEM((SUBLANES, 2 * D_FF), F32),
                        pltpu.VMEM((tm + SUBLANES, tf), F32)],
        compiler_params=_cparams("arbitrary"),
        name="conv_ffn",
    )(h, g1.reshape(1, D_MODEL), wup, cw8, cb.reshape(1, 2 * D_FF), wdn, g2.reshape(1, D_MODEL))


def _s5_kernel(u_ref, bmat_ref, are_ref, aim_ref, cmat_ref, d_ref, wg_ref, bg_ref, o_ref,
               h_ref, bu_ref, hs_ref, ys_ref, *, nb, q):
    @pl.when(pl.program_id(0) == 0)
    def _():
        h_ref[...] = jnp.zeros_like(h_ref)

    nt = S5_NS // LANES
    lt = lambda j: slice(j * LANES, (j + 1) * LANES)
    for b in range(nb):
        bu = _dot(u_ref[b], bmat_ref[...])
        for j in range(2 * nt):
            bu_ref[j, b * q:(b + 1) * q, :] = bu[:, lt(j)]
    a_re = [jnp.broadcast_to(are_ref[:, lt(j)], (nb, LANES)) for j in range(nt)]
    a_im = [jnp.broadcast_to(aim_ref[:, lt(j)], (nb, LANES)) for j in range(nt)]
    h_re = [h_ref[:, lt(j)] for j in range(nt)]
    h_im = [h_ref[:, lt(nt + j)] for j in range(nt)]
    for t in range(q):
        rows = slice(t * nb, (t + 1) * nb)
        for j in range(nt):
            n_re = a_re[j] * h_re[j] - a_im[j] * h_im[j] + bu_ref[j, pl.ds(t, nb, stride=q), :]
            n_im = a_re[j] * h_im[j] + a_im[j] * h_re[j] + bu_ref[nt + j, pl.ds(t, nb, stride=q), :]
            h_re[j], h_im[j] = n_re, n_im
            hs_ref[rows, lt(j)] = n_re
            hs_ref[rows, lt(nt + j)] = n_im
    for j in range(nt):
        h_ref[:, lt(j)] = h_re[j]
        h_ref[:, lt(nt + j)] = h_im[j]
    ys = _dot(hs_ref[...], cmat_ref[...])
    for j in range(S5_WIDTH // LANES):
        ys_ref[j] = ys[:, lt(j)]
    for b in range(nb):
        y = jnp.concatenate([ys_ref[j, pl.ds(b, q, stride=nb), :] for j in range(S5_WIDTH // LANES)], axis=1)
        y = _gelu_tanh(y + d_ref[...] * u_ref[b])
        o_ref[b] = y * _sigmoid(_dot(y, wg_ref[...]) + bg_ref[...])


def _s5_tables(lam_re, lam_im, log_dt, b_re, b_im, c_re, c_im):
    lr, li = lam_re.astype(F32), lam_im.astype(F32)
    dt = jnp.exp(log_dt.astype(F32))[:, None]
    mag = jnp.exp(lr * dt)
    ab_re, ab_im = mag * jnp.cos(li * dt), mag * jnp.sin(li * dt)
    den = lr * lr + li * li
    zr, zi = ab_re - 1.0, ab_im
    f_re = (zr * lr + zi * li) / den
    f_im = (zi * lr - zr * li) / den
    br, bi = b_re.astype(F32), b_im.astype(F32)
    bb_re = f_re[..., None] * br - f_im[..., None] * bi
    bb_im = f_re[..., None] * bi + f_im[..., None] * br
    eye = jnp.eye(S5_GROUPS, dtype=F32)
    to_b = lambda t: jnp.einsum("gpc,gh->gchp", t, eye).reshape(S5_WIDTH, S5_NS)
    bmat = jnp.concatenate([to_b(bb_re), to_b(bb_im)], axis=1)
    to_c = lambda t: jnp.einsum("gcp,gh->gphc", t.astype(F32), eye).reshape(S5_NS, S5_WIDTH)
    cmat = jnp.concatenate([to_c(c_re), -to_c(c_im)], axis=0)
    return bmat.astype(BF16), ab_re.reshape(1, S5_NS), ab_im.reshape(1, S5_NS), cmat.astype(BF16)


def _s5(u, tables, d_skip, w_glu, b_glu, nb, lp, q):
    bmat, a_re, a_im, cmat = tables
    u3 = u.reshape(nb, lp, S5_WIDTH)
    out = pl.pallas_call(
        functools.partial(_s5_kernel, nb=nb, q=q),
        grid=(lp // q,),
        in_specs=[pl.BlockSpec((nb, q, S5_WIDTH), lambda c: (0, c, 0)),
                  _const_spec((S5_WIDTH, 2 * S5_NS)),
                  _const_spec((1, S5_NS)),
                  _const_spec((1, S5_NS)),
                  _const_spec((2 * S5_NS, S5_WIDTH)),
                  _const_spec((1, S5_WIDTH)),
                  _const_spec((S5_WIDTH, S5_WIDTH)),
                  _const_spec((1, S5_WIDTH))],
        out_specs=pl.BlockSpec((nb, q, S5_WIDTH), lambda c: (0, c, 0)),
        out_shape=jax.ShapeDtypeStruct((nb, lp, S5_WIDTH), F32),
        scratch_shapes=[pltpu.VMEM((nb, 2 * S5_NS), F32),
                        pltpu.VMEM((2 * S5_NS // LANES, nb * q, LANES), F32),
                        pltpu.VMEM((nb * q, 2 * S5_NS), F32),
                        pltpu.VMEM((S5_WIDTH // LANES, nb * q, LANES), F32)],
        compiler_params=_cparams("arbitrary"),
        name="s5",
    )(u3, bmat, a_re, a_im, cmat, d_skip.reshape(1, S5_WIDTH).astype(F32), w_glu.astype(BF16),
      b_glu.reshape(1, S5_WIDTH).astype(F32))
    return out.reshape(nb * lp, S5_WIDTH)


def _ssd_kernel(z_ref, xbc_ref, dt_ref, cw_ref, cb_ref, dtb_ref, alog_ref, dsk_ref, nw_ref, o_ref,
                carry_ref, ext_ref, st_ref, y_ref):
    c = pl.program_id(1)

    @pl.when(c == 0)
    def _():
        carry_ref[...] = jnp.zeros_like(carry_ref)
        st_ref[...] = jnp.zeros_like(st_ref)

    x = xbc_ref[...]
    ext_ref[0:SUBLANES, :] = carry_ref[...]
    ext_ref[SUBLANES:SUBLANES + CHUNK, :] = x
    carry_ref[...] = x[CHUNK - SUBLANES:CHUNK, :]
    cw = cw_ref[...]
    conv = cb_ref[...] + cw[SSD_CONV - 1:SSD_CONV] * x
    for k in range(SSD_CONV - 1):
        off = SUBLANES - (SSD_CONV - 1) + k
        conv = conv + cw[k:k + 1] * ext_ref[off:off + CHUNK, :]
    xa = _silu(conv)

    row = lax.broadcasted_iota(jnp.int32, (CHUNK, 1), 0)
    dt = _softplus(dt_ref[...] + dtb_ref[...])
    dt = jnp.where(jnp.logical_or(c > 0, row >= PAD), dt, 0.0)
    d = dt * (-jnp.exp(alog_ref[...]))
    li = lax.broadcasted_iota(jnp.int32, (CHUNK, CHUNK), 0)
    si = lax.broadcasted_iota(jnp.int32, (CHUNK, CHUNK), 1)
    causal = li >= si
    acum = _dot_sel_lhs(causal.astype(BF16), d)
    acum_t = _dot_sel_rhs(d.T, (li <= si).astype(BF16))

    for g in range(SSD_GROUPS):
        bg = xa[:, SSD_INNER + g * SSD_STATE:SSD_INNER + (g + 1) * SSD_STATE]
        cg = xa[:, SSD_INNER + (SSD_GROUPS + g) * SSD_STATE:SSD_INNER + (SSD_GROUPS + g + 1) * SSD_STATE]
        cb = _dot_nt(cg, bg)
        bg_t = bg.T
        for j in range(SSD_HPG):
            hd = g * SSD_HPG + j
            sl = slice(hd * SSD_HEADDIM, (hd + 1) * SSD_HEADDIM)
            xs = xa[:, sl]
            ac = acum[:, hd:hd + 1]
            seg = jnp.exp(jnp.minimum(ac - acum_t[hd:hd + 1, :], 0.0))
            m = jnp.where(causal, cb * seg, 0.0)
            xdt = xs * dt[:, hd:hd + 1]
            st = st_ref[hd]
            y = _dot(m, xdt) + _dot(cg, st) * jnp.exp(ac) + xs * dsk_ref[:, sl]
            y_ref[:, sl] = y
            a_last = acum[CHUNK - 1:CHUNK, hd:hd + 1]
            st_ref[hd] = st * jnp.exp(a_last) + _dot(bg_t, xdt * jnp.exp(a_last - ac))

    y = y_ref[...] * _silu(z_ref[...])
    gw = SSD_INNER // SSD_GROUPS
    for g in range(SSD_GROUPS):
        o_ref[:, g * gw:(g + 1) * gw] = _rms(y[:, g * gw:(g + 1) * gw], nw_ref[:, g * gw:(g + 1) * gw])


def _ssd(z, xbc, dtr, conv_w, conv_b, dt_bias, a_log, d_skip, norm_w, nb, lp):
    pad_h = lambda t: jnp.zeros((1, LANES), F32).at[0, :SSD_HEADS].set(t.astype(F32))
    cw8 = jnp.zeros((SUBLANES, SSD_XBC), F32).at[:SSD_CONV].set(conv_w.astype(F32))
    dsk = jnp.repeat(d_skip.astype(F32), SSD_HEADDIM).reshape(1, SSD_INNER)
    blk = lambda w: pl.BlockSpec((None, CHUNK, w), lambda b, c: (b, c, 0))
    out = pl.pallas_call(
        _ssd_kernel,
        grid=(nb, lp // CHUNK),
        in_specs=[blk(SSD_INNER), blk(SSD_XBC), blk(LANES),
                  _const_spec((SUBLANES, SSD_XBC)), _const_spec((1, SSD_XBC)),
                  _const_spec((1, LANES)), _const_spec((1, LANES)),
                  _const_spec((1, SSD_INNER)), _const_spec((1, SSD_INNER))],
        out_specs=blk(SSD_INNER),
        out_shape=jax.ShapeDtypeStruct((nb, lp, SSD_INNER), F32),
        scratch_shapes=[pltpu.VMEM((SUBLANES, SSD_XBC), F32),
                        pltpu.VMEM((CHUNK + SUBLANES, SSD_XBC), F32),
                        pltpu.VMEM((SSD_HEADS, SSD_STATE, SSD_HEADDIM), F32),
                        pltpu.VMEM((CHUNK, SSD_INNER), F32)],
        compiler_params=_cparams("arbitrary", "arbitrary"),
        name="ssd",
    )(z.reshape(nb, lp, SSD_INNER), xbc.reshape(nb, lp, SSD_XBC), dtr.reshape(nb, lp, LANES),
      cw8, conv_b.reshape(1, SSD_XBC).astype(F32), pad_h(dt_bias), pad_h(a_log), dsk,
      norm_w.reshape(1, SSD_INNER).astype(F32))
    return out.reshape(nb * lp, SSD_INNER)


HG_LEVELS = 6


def _hgrn_tables():
    e = np.zeros((2 + HG_LEVELS, CHUNK, CHUNK), np.float32)
    m = np.zeros((1 + HG_LEVELS, CHUNK, CHUNK), np.float32)
    idx = np.arange(CHUNK)
    e[0] = idx[None, :] <= idx[:, None]
    e[1] = idx[None, :] > idx[:, None]
    m[0] = np.eye(CHUNK)
    for k in range(1, HG_LEVELS + 1):
        half = 1 << (k - 1)
        for r in range(CHUNK):
            bound = ((r >> k) << k) + half - 1
            if r > bound:
                e[1 + k, r, bound + 1:r + 1] = 1.0
            else:
                e[1 + k, r, r + 1:bound + 1] = 1.0
        same = (idx[:, None] >> k) == (idx[None, :] >> k)
        upper = ((idx[:, None] >> (k - 1)) & 1) == 1
        lower = ((idx[None, :] >> (k - 1)) & 1) == 0
        m[k] = same & upper & lower
    return e.reshape(-1, CHUNK), m


def _hgrn_kernel(p_ref, lb_ref, nw_ref, e_ref, m_ref, o_ref, st_ref):
    @pl.when(pl.program_id(1) == 0)
    def _():
        st_ref[...] = jnp.zeros_like(st_ref)

    hd = HGRN_HEADDIM
    for h in range(HGRN_HEADS):
        q = _silu(p_ref[:, h * hd:(h + 1) * hd])
        f = p_ref[:, HGRN_WIDTH + h * hd:HGRN_WIDTH + (h + 1) * hd]
        iv = p_ref[:, 2 * HGRN_WIDTH + h * hd:2 * HGRN_WIDTH + (h + 1) * hd]
        og = p_ref[:, 3 * HGRN_WIDTH + h * hd:3 * HGRN_WIDTH + (h + 1) * hd]
        lb = lb_ref[:, h * hd:(h + 1) * hd]
        forget = lb + (1.0 - lb) * _sigmoid(f)
        kf = 1.0 - forget
        eg = jnp.exp(_dot_sel_lhs(e_ref[...], jnp.log(forget)))
        att = m_ref[0] * _dot_nt(q, kf)
        for k in range(1, HG_LEVELS + 1):
            ek = eg[(1 + k) * CHUNK:(2 + k) * CHUNK]
            att = att + m_ref[k] * _dot_nt(q * ek, kf * ek)
        st = st_ref[h]
        out = _dot(att, iv) + _dot_nt(q * eg[0:CHUNK], st)
        st_ref[h] = st * eg[CHUNK - 1:CHUNK] + _dot_tn(iv, kf * eg[CHUNK:2 * CHUNK])
        out = _rms(out, nw_ref[:, h * hd:(h + 1) * hd])
        o_ref[:, h * hd:(h + 1) * hd] = out * _silu(og)


def _hgrn(p, lb, norm_w, nb, lp):
    e_np, m_np = _hgrn_tables()
    blk = lambda w: pl.BlockSpec((None, CHUNK, w), lambda b, c: (b, c, 0))
    out = pl.pallas_call(
        _hgrn_kernel,
        grid=(nb, lp // CHUNK),
        in_specs=[blk(HGRN_IN), _const_spec((1, HGRN_WIDTH)), _const_spec((1, HGRN_WIDTH)),
                  _const_spec(e_np.shape), _const_spec(m_np.shape)],
        out_specs=blk(HGRN_WIDTH),
        out_shape=jax.ShapeDtypeStruct((nb, lp, HGRN_WIDTH), F32),
        scratch_shapes=[pltpu.VMEM((HGRN_HEADS, HGRN_HEADDIM, HGRN_HEADDIM), F32)],
        compiler_params=_cparams("arbitrary", "arbitrary"),
        name="hgrn2",
    )(p.reshape(nb, lp, HGRN_IN), lb.reshape(1, HGRN_WIDTH).astype(F32),
      norm_w.reshape(1, HGRN_WIDTH).astype(F32), jnp.asarray(e_np, BF16), jnp.asarray(m_np, F32))
    return out.reshape(nb * lp, HGRN_WIDTH)


def _head_ones():
    idx = np.arange(RWKV_WIDTH) // RWKV_HEADDIM
    return (idx[:, None] == idx[None, :]).astype(np.float32)


def _rwkv_prep_kernel(*refs, tm, tiles_per_batch, has_vres):
    if has_vres:
        (p_ref, mu_ref, w0_ref, w2_ref, a0_ref, a2_ref, g2_ref, kk_ref, ka_ref, rk_ref, ones_ref,
         vf_ref, v0_ref, v2_ref,
         r_o, w_o, k_o, v_o, kkn_o, b_o, g_o, bv_o, carry_ref, ext_ref) = refs
    else:
        (p_ref, mu_ref, w0_ref, w2_ref, a0_ref, a2_ref, g2_ref, kk_ref, ka_ref, rk_ref, ones_ref,
         r_o, w_o, k_o, v_o, kkn_o, b_o, g_o, bv_o, carry_ref, ext_ref) = refs

    @pl.when(pl.program_id(0) % tiles_per_batch == 0)
    def _():
        carry_ref[...] = jnp.zeros_like(carry_ref)

    p = p_ref[...]
    ext_ref[0:SUBLANES, :] = carry_ref[...]
    ext_ref[SUBLANES:SUBLANES + tm, :] = p
    carry_ref[...] = p[tm - SUBLANES:tm, :]
    prev = ext_ref[SUBLANES - 1:SUBLANES - 1 + tm, :]
    ps = p + (prev - p) * mu_ref[...]

    wd = RWKV_WIDTH
    r, k, v = ps[:, :wd], ps[:, wd:2 * wd], ps[:, 2 * wd:3 * wd]
    pwa = ps[:, 3 * wd:3 * wd + LORA_W + LORA_A]
    pg = ps[:, 3 * wd + LORA_W + LORA_A:RWKV_IN]
    w_log = -_softplus(-(w0_ref[...] + _dot(jnp.tanh(pwa), w2_ref[...]))) - 0.5
    w_o[...] = jnp.exp(-jnp.exp(w_log))
    a = _sigmoid(a0_ref[...] + _dot(pwa, a2_ref[...]))
    if has_vres:
        pv = ps[:, RWKV_IN:RWKV_INP]
        v = v + (vf_ref[...] - v) * _sigmoid(v0_ref[...] + _dot(pv, v2_ref[...]))
    g_o[...] = _dot(_sigmoid(pg), g2_ref[...])
    kk = k * kk_ref[...]
    ss = _dot_sel_rhs(kk * kk, ones_ref[...])
    kk = kk * lax.rsqrt(jnp.maximum(ss, 1e-24))
    k2 = k * (1.0 + (a - 1.0) * ka_ref[...])
    bonus = _dot_sel_rhs(r * k2 * rk_ref[...], ones_ref[...])
    r_o[...] = r
    k_o[...] = k2
    v_o[...] = v
    kkn_o[...] = kk
    b_o[...] = kk * a
    bv_o[...] = bonus * v


def _rwkv_prep(p, mu, w0, w2, a0, a2, g2, k_k, k_a, r_k, vres, tm, lp):
    rows = p.shape[0]
    wd = RWKV_WIDTH
    row = lambda t: t.reshape(1, -1).astype(F32)
    w2p = jnp.zeros((LORA_W + LORA_A, wd), F32).at[:LORA_W].set(w2).astype(BF16)
    a2p = jnp.zeros((LORA_W + LORA_A, wd), F32).at[LORA_W:].set(a2).astype(BF16)
    args = [p, row(mu), row(w0), w2p, row(a0), a2p, g2.astype(BF16), row(k_k), row(k_a), row(r_k),
            jnp.asarray(_head_ones(), BF16)]
    specs = [pl.BlockSpec((tm, RWKV_INP), lambda i: (i, 0)), _const_spec((1, RWKV_INP)),
             _const_spec((1, wd)), _const_spec((LORA_W + LORA_A, wd)), _const_spec((1, wd)),
             _const_spec((LORA_W + LORA_A, wd)), _const_spec((LORA_G, wd)), _const_spec((1, wd)),
             _const_spec((1, wd)), _const_spec((1, wd)), _const_spec((wd, wd))]
    if vres is not None:
        v_first, v0, v2 = vres
        v2p = jnp.zeros((RWKV_INP - RWKV_IN, wd), F32).at[:LORA_V].set(v2).astype(BF16)
        args += [v_first, row(v0), v2p]
        specs += [pl.BlockSpec((tm, wd), lambda i: (i, 0)), _const_spec((1, wd)),
                  _const_spec((RWKV_INP - RWKV_IN, wd))]
    return pl.pallas_call(
        functools.partial(_rwkv_prep_kernel, tm=tm, tiles_per_batch=lp // tm, has_vres=vres is not None),
        grid=(rows // tm,),
        in_specs=specs,
        out_specs=[pl.BlockSpec((tm, wd), lambda i: (i, 0))] * 8,
        out_shape=[jax.ShapeDtypeStruct((rows, wd), F32)] * 8,
        scratch_shapes=[pltpu.VMEM((SUBLANES, RWKV_INP), F32),
                        pltpu.VMEM((tm + SUBLANES, RWKV_INP), F32)],
        compiler_params=_cparams("arbitrary"),
        name="rwkv_prep",
    )(*args)


RW_KLO = RWKV_HEADDIM // 2


def _rwkv_scan_kernel(r_ref, w_ref, k_ref, kk_ref, b_ref, v_ref, o_ref, m_ref, *, tb, nl):
    @pl.when(pl.program_id(0) == 0)
    def _():
        m_ref[...] = jnp.zeros_like(m_ref)

    shape = (RWKV_HEADDIM, nl)

    def step(t, carry):
        vt = v_ref[t]
        kk_t = kk_ref[t]
        sa = jnp.zeros(shape, F32)
        for kl in range(RW_KLO):
            sa = sa + m_ref[kl] * jnp.broadcast_to(kk_t[kl:kl + 1], shape)
        sa = sa + pltpu.roll(sa, nl // 2, 1)
        w_t, b_t, k_t, r_t = w_ref[t], b_ref[t], k_ref[t], r_ref[t]
        o = jnp.zeros(shape, F32)
        for kl in range(RW_KLO):
            bc = lambda x: jnp.broadcast_to(x[kl:kl + 1], shape)
            mk = m_ref[kl] * bc(w_t) - sa * bc(b_t) + vt * bc(k_t)
            m_ref[kl] = mk
            o = o + mk * bc(r_t)
        o_ref[t] = o + pltpu.roll(o, nl // 2, 1)
        return carry

    lax.fori_loop(0, tb, step, 0)


def _rwkv_scan(r, w, k2, kk, b, v, nb, lp, tb):
    nh, hd = RWKV_HEADS, RWKV_HEADDIM
    nl = 2 * nb * nh

    def k_layout(t):
        t = t.reshape(nb, lp, nh, 2, RW_KLO)
        return t.transpose(1, 4, 3, 0, 2).reshape(lp, RW_KLO, nl)

    vt = v.reshape(nb, lp, nh, hd).transpose(1, 3, 0, 2).reshape(lp, hd, nb * nh)
    vt = jnp.concatenate([vt, vt], axis=-1)
    kspec = pl.BlockSpec((tb, RW_KLO, nl), lambda i: (i, 0, 0))
    vspec = pl.BlockSpec((tb, hd, nl), lambda i: (i, 0, 0))
    o = pl.pallas_call(
        functools.partial(_rwkv_scan_kernel, tb=tb, nl=nl),
        grid=(lp // tb,),
        in_specs=[kspec] * 5 + [vspec],
        out_specs=vspec,
        out_shape=jax.ShapeDtypeStruct((lp, hd, nl), F32),
        scratch_shapes=[pltpu.VMEM((RW_KLO, hd, nl), F32)],
        compiler_params=_cparams("arbitrary"),
        name="rwkv_scan",
    )(k_layout(r), k_layout(w), k_layout(k2), k_layout(kk), k_layout(b), vt)
    o = o[:, :, :nb * nh].reshape(lp, hd, nb, nh).transpose(2, 0, 3, 1)
    return o.reshape(nb * lp, RWKV_WIDTH)


def _rwkv_post_kernel(o_ref, bv_ref, g_ref, lw_ref, lb_ref, ones_ref, y_ref):
    o = o_ref[...]
    inv = 1.0 / RWKV_HEADDIM
    mean = _dot_sel_rhs(o, ones_ref[...]) * inv
    xc = o - mean
    var = _dot_sel_rhs(xc * xc, ones_ref[...]) * inv
    y = xc * lax.rsqrt(var + GN_EPS) * lw_ref[...] + lb_ref[...] + bv_ref[...]
    y_ref[...] = y * g_ref[...]


def _rwkv_post(o, bv, g, ln_w, ln_b, tm):
    rows = o.shape[0]
    wd = RWKV_WIDTH
    blk = pl.BlockSpec((tm, wd), lambda i: (i, 0))
    return pl.pallas_call(
        _rwkv_post_kernel,
        grid=(rows // tm,),
        in_specs=[blk, blk, blk, _const_spec((1, wd)), _const_spec((1, wd)), _const_spec((wd, wd))],
        out_specs=blk,
        out_shape=jax.ShapeDtypeStruct((rows, wd), F32),
        compiler_params=_cparams("arbitrary"),
        name="rwkv_post",
    )(o, bv, g, ln_w.reshape(1, wd).astype(F32), ln_b.reshape(1, wd).astype(F32),
      jnp.asarray(_head_ones(), BF16))


def _even_weight(w_in):
    n = w_in.shape[1]
    padw = jnp.zeros((D_MODEL, LANES - SSD_HEADS), w_in.dtype)
    return jnp.concatenate([w_in, padw], axis=1).astype(BF16), n - SSD_HEADS


def kernel(x, meta, norm_mix_pre, norm_mix_post, norm_ffn_pre, norm_ffn_post, mix_w_out, ffn_w_up, ffn_conv_w, ffn_conv_b, ffn_w_down, ev_w_in, s5_lam_re, s5_lam_im, s5_log_dt, s5_b_re, s5_b_im, s5_c_re, s5_c_im, s5_d, s5_w_glu, s5_b_glu, ssd_conv_w, ssd_conv_b, ssd_dt_bias, ssd_a_log, ssd_d, ssd_norm, od_w_in, rw_mu, rw_w0, rw_w2, rw_a0, rw_a2, rw_g2, rw_k_k, rw_k_a, rw_r_k, rw_ln_w, rw_ln_b, rw_w_vin, rw_mu_v, rw_v0, rw_v2, hg_lb_raw, hg_norm):
    nb, seq, _ = x.shape
    depth = norm_mix_pre.shape[0]
    lp = PAD + N_META + seq
    rows = nb * lp
    tm = _row_tile(lp, 520)
    h = jnp.concatenate([jnp.zeros((nb, PAD, D_MODEL), x.dtype),
                         jnp.broadcast_to(meta.astype(x.dtype)[None], (nb, N_META, D_MODEL)), x], axis=1)
    h = h.reshape(rows, D_MODEL)

    lb_w = jax.nn.softmax(hg_lb_raw.astype(F32), axis=0)
    lb_table = jnp.cumsum(lb_w, axis=0) - lb_w[0]
    v_first = None
    s1 = S5_WIDTH
    s2 = s1 + SSD_INNER
    s3 = s2 + SSD_XBC
    for layer in range(depth):
        if layer % 2 == 0:
            e = layer // 2
            w_in, _ = _even_weight(ev_w_in[e])
            u, z, xbc, dtr = _norm_proj(h, norm_mix_pre[layer], w_in,
                                        ((0, s1), (s1, SSD_INNER), (s2, SSD_XBC), (s3, LANES)), tm)
            tables = _s5_tables(s5_lam_re[e], s5_lam_im[e], s5_log_dt[e], s5_b_re[e], s5_b_im[e],
                                s5_c_re[e], s5_c_im[e])
            y_a = _s5(u, tables, s5_d[e], s5_w_glu[e], s5_b_glu[e], nb, lp, CHUNK)
            y_b = _ssd(z, xbc, dtr, ssd_conv_w[e], ssd_conv_b[e], ssd_dt_bias[e], ssd_a_log[e], ssd_d[e],
                       ssd_norm[e], nb, lp)
        else:
            o = layer // 2
            mu = jnp.zeros((RWKV_INP,), F32).at[:RWKV_IN].set(rw_mu[o])
            w_rw = jnp.zeros((D_MODEL, RWKV_INP), F32).at[:, :RWKV_IN].set(od_w_in[o][:, :RWKV_IN])
            if o > 0:
                mu = mu.at[RWKV_IN:RWKV_IN + LORA_V].set(rw_mu_v[o - 1])
                w_rw = w_rw.at[:, RWKV_IN:RWKV_IN + LORA_V].set(rw_w_vin[o - 1])
            w_in = jnp.concatenate([w_rw, od_w_in[o][:, RWKV_IN:]], axis=1).astype(BF16)
            p_rw, p_hg = _norm_proj(h, norm_mix_pre[layer], w_in, ((0, RWKV_INP), (RWKV_INP, HGRN_IN)), tm)
            vres = None if o == 0 else (v_first, rw_v0[o - 1], rw_v2[o - 1])
            r, w, k2, v, kk, b, g, bv = _rwkv_prep(p_rw, mu, rw_w0[o], rw_w2[o], rw_a0[o], rw_a2[o],
                                                   rw_g2[o], rw_k_k[o], rw_k_a[o], rw_r_k[o], vres, tm, lp)
            if o == 0:
                v_first = v
            o_rw = _rwkv_scan(r, w, k2, kk, b, v, nb, lp, 16)
            y_a = _rwkv_post(o_rw, bv, g, rw_ln_w[o], rw_ln_b[o], tm)
            y_b = _hgrn(p_hg, lb_table[o], hg_norm[o], nb, lp)
        ka = y_a.shape[1]
        w_out = mix_w_out[layer].astype(BF16)
        h = _out_proj(y_a, y_b, w_out[:ka], w_out[ka:], norm_mix_post[layer], h, tm, lp)
        h = _ffn(h, norm_ffn_pre[layer], ffn_w_up[layer].astype(BF16), ffn_conv_w[layer], ffn_conv_b[layer],
                 ffn_w_down[layer].astype(BF16), norm_ffn_post[layer], tm, lp)
    return h.reshape(nb, lp, D_MODEL)[:, PAD + N_META:]
```

```python
import functools

import numpy as np
import jax
import jax.numpy as jnp
from jax import lax
from jax.experimental import pallas as pl
from jax.experimental.pallas import tpu as pltpu

F32 = jnp.float32
BF16 = jnp.bfloat16

D_MODEL = 1024
N_META = 16
CHUNK = 64
PAD = CHUNK - N_META
RMS_EPS = 1e-6

S5_WIDTH = 256
S5_GROUP = 16
S5_GROUPS = S5_WIDTH // S5_GROUP
S5_STATE = 64
S5_NS = S5_GROUPS * S5_STATE

SSD_HEADDIM = 64
SSD_INNER = 768
SSD_HEADS = SSD_INNER // SSD_HEADDIM
SSD_GROUPS = 2
SSD_HPG = SSD_HEADS // SSD_GROUPS
SSD_STATE = 128
SSD_CONV = 4
SSD_XBC = SSD_INNER + 2 * SSD_GROUPS * SSD_STATE
MIX_WIDTH = S5_WIDTH + SSD_INNER

RWKV_WIDTH = 512
RWKV_HEADDIM = 64
RWKV_HEADS = RWKV_WIDTH // RWKV_HEADDIM
LORA_W = 64
LORA_A = 64
LORA_V = 32
LORA_G = 128
GN_EPS = 64e-5
RWKV_IN = 3 * RWKV_WIDTH + LORA_W + LORA_A + LORA_G
RWKV_INP = RWKV_IN + 128

HGRN_WIDTH = 512
HGRN_HEADS = 4
HGRN_HEADDIM = HGRN_WIDTH // HGRN_HEADS
HGRN_IN = 4 * HGRN_WIDTH

D_FF = 2816
FFN_CONV = 3

LANES = 128
SUBLANES = 8
VMEM_LIMIT = 56 * 1024 * 1024


def _cparams(*sem):
    return pltpu.CompilerParams(dimension_semantics=sem, vmem_limit_bytes=VMEM_LIMIT)


def _row_tile(rows, target):
    best = SUBLANES
    for t in range(SUBLANES, min(rows, target) + 1, SUBLANES):
        if rows % t == 0:
            best = t
    return best


def _const_spec(shape):
    nd = len(shape)
    return pl.BlockSpec(shape, lambda *_: (0,) * nd, pipeline_mode=pl.Buffered(1))


def _dot(a, b):
    return jnp.dot(a.astype(BF16), b.astype(BF16), preferred_element_type=F32)


def _dot_nt(a, b):
    return lax.dot_general(a.astype(BF16), b.astype(BF16), (((1,), (1,)), ((), ())),
                           preferred_element_type=F32)


def _dot_tn(a, b):
    return lax.dot_general(a.astype(BF16), b.astype(BF16), (((0,), (0,)), ((), ())),
                           preferred_element_type=F32)


def _split3(x):
    hi = x.astype(BF16)
    r1 = x - hi.astype(F32)
    mid = r1.astype(BF16)
    lo = (r1 - mid.astype(F32)).astype(BF16)
    return hi, mid, lo


def _dot_sel_lhs(sel, x):
    hi, mid, lo = _split3(x)
    d = lambda p: jnp.dot(sel, p, preferred_element_type=F32)
    return d(hi) + d(mid) + d(lo)


def _dot_sel_rhs(x, sel):
    hi, mid, lo = _split3(x)
    d = lambda p: jnp.dot(p, sel, preferred_element_type=F32)
    return d(hi) + d(mid) + d(lo)


def _sigmoid(x):
    return 1.0 / (1.0 + jnp.exp(-x))


def _silu(x):
    return x * _sigmoid(x)


def _softplus(x):
    return jnp.maximum(x, 0.0) + jnp.log(1.0 + jnp.exp(-jnp.abs(x)))


GELU_C = 0.7978845608028654
GELU_A = 0.044715


def _gelu_tanh(x):
    return 0.5 * x * (1.0 + jnp.tanh(GELU_C * (x + GELU_A * (x * x * x))))


def _rms(x, g):
    return x * lax.rsqrt(jnp.mean(x * x, axis=-1, keepdims=True) + RMS_EPS) * g


def _keep_rows(tm, nb):
    row = pl.program_id(0) * tm + lax.broadcasted_iota(jnp.int32, (tm, 1), 0)
    return row >= PAD * nb


def _norm_proj_kernel(x_ref, g_ref, w_ref, *o_refs, splits, chunk):
    hn = _rms(x_ref[...], g_ref[...]).astype(BF16)
    for o_ref, (start, width) in zip(o_refs, splits):
        for c0 in range(0, width, chunk):
            cw = min(chunk, width - c0)
            o_ref[:, c0:c0 + cw] = jnp.dot(hn, w_ref[:, start + c0:start + c0 + cw],
                                           preferred_element_type=F32)


def _norm_proj(h, g, w, splits, tm):
    rows = h.shape[0]
    n = w.shape[1]
    return pl.pallas_call(
        functools.partial(_norm_proj_kernel, splits=splits, chunk=512),
        grid=(rows // tm,),
        in_specs=[pl.BlockSpec((tm, D_MODEL), lambda i: (i, 0)),
                  _const_spec((1, D_MODEL)),
                  _const_spec((D_MODEL, n))],
        out_specs=[pl.BlockSpec((tm, wd), lambda i: (i, 0)) for _, wd in splits],
        out_shape=[jax.ShapeDtypeStruct((rows, wd), F32) for _, wd in splits],
        compiler_params=_cparams("arbitrary"),
        name="norm_proj",
    )(h, g.reshape(1, D_MODEL), w)


def _out_proj_kernel(ya_ref, yb_ref, wa_ref, wb_ref, g_ref, h_ref, o_ref, *, tm, nb):
    o = _dot(ya_ref[...], wa_ref[...]) + _dot(yb_ref[...], wb_ref[...])
    upd = _rms(o, g_ref[...])
    o_ref[...] = h_ref[...] + jnp.where(_keep_rows(tm, nb), upd, 0.0)


def _out_proj(ya, yb, wa, wb, g, h, tm, nb):
    rows = h.shape[0]
    ka, kb = ya.shape[1], yb.shape[1]
    return pl.pallas_call(
        functools.partial(_out_proj_kernel, tm=tm, nb=nb),
        grid=(rows // tm,),
        in_specs=[pl.BlockSpec((tm, ka), lambda i: (i, 0)),
                  pl.BlockSpec((tm, kb), lambda i: (i, 0)),
                  _const_spec((ka, D_MODEL)),
                  _const_spec((kb, D_MODEL)),
                  _const_spec((1, D_MODEL)),
                  pl.BlockSpec((tm, D_MODEL), lambda i: (i, 0))],
        out_specs=pl.BlockSpec((tm, D_MODEL), lambda i: (i, 0)),
        out_shape=jax.ShapeDtypeStruct((rows, D_MODEL), F32),
        compiler_params=_cparams("arbitrary"),
        name="out_proj",
    )(ya, yb, wa, wb, g.reshape(1, D_MODEL), h)


def _ffn_kernel(h_ref, g1_ref, wup_ref, cw_ref, cb_ref, wdn_ref, g2_ref, o_ref, carry_ref, act_ref,
                *, tm, tf, nb):
    halo = (FFN_CONV - 1) * nb

    @pl.when(pl.program_id(0) == 0)
    def _():
        carry_ref[...] = jnp.zeros_like(carry_ref)

    x = h_ref[...]
    hn = _rms(x, g1_ref[...]).astype(BF16)
    for c in range(D_FF // tf):
        halves = []
        for part in range(2):
            col = part * D_FF + c * tf
            u = jnp.dot(hn, wup_ref[:, col:col + tf], preferred_element_type=F32)
            hist = carry_ref[:, col:col + tf]
            carry_ref[:, col:col + tf] = u[tm - halo:tm, :]
            prev2 = jnp.concatenate([hist, u[:tm - halo, :]], axis=0)
            prev1 = jnp.concatenate([hist[nb:, :], u[:tm - nb, :]], axis=0)
            cw = cw_ref[:, col:col + tf]
            halves.append(cw[0:1] * prev2 + cw[1:2] * prev1 + cw[2:3] * u + cb_ref[:, col:col + tf])
        gate, val = halves
        th = jnp.tanh(gate * (GELU_C + (GELU_C * GELU_A) * (gate * gate)))
        act_ref[:, c * tf:(c + 1) * tf] = (gate * (0.5 + 0.5 * th) * val).astype(BF16)
    acc = jnp.dot(act_ref[...], wdn_ref[...], preferred_element_type=F32)
    upd = _rms(acc, g2_ref[...])
    o_ref[...] = x + jnp.where(_keep_rows(tm, nb), upd, 0.0)


def _ffn(h, g1, wup, cw, cb, wdn, g2, tm, nb):
    rows = h.shape[0]
    tf = 256
    halo = (FFN_CONV - 1) * nb
    cw8 = jnp.zeros((SUBLANES, 2 * D_FF), F32).at[:FFN_CONV].set(cw)
    return pl.pallas_call(
        functools.partial(_ffn_kernel, tm=tm, tf=tf, nb=nb),
        grid=(rows // tm,),
        in_specs=[pl.BlockSpec((tm, D_MODEL), lambda i: (i, 0)),
                  _const_spec((1, D_MODEL)),
                  _const_spec((D_MODEL, 2 * D_FF)),
                  _const_spec((SUBLANES, 2 * D_FF)),
                  _const_spec((1, 2 * D_FF)),
                  _const_spec((D_FF, D_MODEL)),
                  _const_spec((1, D_MODEL))],
        out_specs=pl.BlockSpec((tm, D_MODEL), lambda i: (i, 0)),
        out_shape=jax.ShapeDtypeStruct((rows, D_MODEL), F32),
        scratch_shapes=[pltpu.VMEM((halo, 2 * D_FF), F32),
                        pltpu.VMEM((tm, D_FF), BF16)],
        compiler_params=_cparams("arbitrary"),
        name="conv_ffn",
    )(h, g1.reshape(1, D_MODEL), wup, cw8, cb.reshape(1, 2 * D_FF), wdn, g2.reshape(1, D_MODEL))


def _s5_kernel(u_ref, bmat_ref, are_ref, aim_ref, cmat_ref, d_ref, wg_ref, bg_ref, o_ref,
               h_ref, bu_ref, hs_ref, *, nb, q):
    @pl.when(pl.program_id(0) == 0)
    def _():
        h_ref[...] = jnp.zeros_like(h_ref)

    u = u_ref[...]
    bu_ref[...] = _dot(u, bmat_ref[...])
    a_re = jnp.broadcast_to(are_ref[...], (nb, S5_NS))
    a_im = jnp.broadcast_to(aim_ref[...], (nb, S5_NS))

    def step(t, carry):
        h_re, h_im = carry
        rows = pl.ds(pl.multiple_of(t * nb, nb), nb)
        n_re = a_re * h_re - a_im * h_im + bu_ref[rows, :S5_NS]
        n_im = a_re * h_im + a_im * h_re + bu_ref[rows, S5_NS:]
        hs_ref[rows, :S5_NS] = n_re
        hs_ref[rows, S5_NS:] = n_im
        return n_re, n_im

    h_re, h_im = lax.fori_loop(0, q, step, (h_ref[:, :S5_NS], h_ref[:, S5_NS:]), unroll=4)
    h_ref[:, :S5_NS] = h_re
    h_ref[:, S5_NS:] = h_im
    y = _gelu_tanh(_dot(hs_ref[...], cmat_ref[...]) + d_ref[...] * u)
    o_ref[...] = y * _sigmoid(_dot(y, wg_ref[...]) + bg_ref[...])


def _s5_tables(lam_re, lam_im, log_dt, b_re, b_im, c_re, c_im):
    lr, li = lam_re.astype(F32), lam_im.astype(F32)
    dt = jnp.exp(log_dt.astype(F32))[:, None]
    mag = jnp.exp(lr * dt)
    ab_re, ab_im = mag * jnp.cos(li * dt), mag * jnp.sin(li * dt)
    den = lr * lr + li * li
    zr, zi = ab_re - 1.0, ab_im
    f_re = (zr * lr + zi * li) / den
    f_im = (zi * lr - zr * li) / den
    br, bi = b_re.astype(F32), b_im.astype(F32)
    bb_re = f_re[..., None] * br - f_im[..., None] * bi
    bb_im = f_re[..., None] * bi + f_im[..., None] * br
    eye = jnp.eye(S5_GROUPS, dtype=F32)
    to_b = lambda t: jnp.einsum("gpc,gh->gchp", t, eye).reshape(S5_WIDTH, S5_NS)
    bmat = jnp.concatenate([to_b(bb_re), to_b(bb_im)], axis=1)
    to_c = lambda t: jnp.einsum("gcp,gh->gphc", t.astype(F32), eye).reshape(S5_NS, S5_WIDTH)
    cmat = jnp.concatenate([to_c(c_re), -to_c(c_im)], axis=0)
    return bmat.astype(BF16), ab_re.reshape(1, S5_NS), ab_im.reshape(1, S5_NS), cmat.astype(BF16)


def _s5(u, tables, d_skip, w_glu, b_glu, nb, lp, q):
    bmat, a_re, a_im, cmat = tables
    blk = pl.BlockSpec((q * nb, S5_WIDTH), lambda c: (c, 0))
    return pl.pallas_call(
        functools.partial(_s5_kernel, nb=nb, q=q),
        grid=(lp // q,),
        in_specs=[blk,
                  _const_spec((S5_WIDTH, 2 * S5_NS)),
                  _const_spec((1, S5_NS)),
                  _const_spec((1, S5_NS)),
                  _const_spec((2 * S5_NS, S5_WIDTH)),
                  _const_spec((1, S5_WIDTH)),
                  _const_spec((S5_WIDTH, S5_WIDTH)),
                  _const_spec((1, S5_WIDTH))],
        out_specs=blk,
        out_shape=jax.ShapeDtypeStruct((lp * nb, S5_WIDTH), F32),
        scratch_shapes=[pltpu.VMEM((nb, 2 * S5_NS), F32),
                        pltpu.VMEM((nb * q, 2 * S5_NS), F32),
                        pltpu.VMEM((nb * q, 2 * S5_NS), F32)],
        compiler_params=_cparams("arbitrary"),
        name="s5",
    )(u, bmat, a_re, a_im, cmat, d_skip.reshape(1, S5_WIDTH).astype(F32), w_glu.astype(BF16),
      b_glu.reshape(1, S5_WIDTH).astype(F32))


def _ssd_kernel(z_ref, xbc_ref, dt_ref, cw_ref, cb_ref, dtb_ref, alog_ref, dsk_ref, nw_ref, o_ref,
                carry_ref, ext_ref, st_ref, y_ref):
    c = pl.program_id(1)

    @pl.when(c == 0)
    def _():
        carry_ref[...] = jnp.zeros_like(carry_ref)
        st_ref[...] = jnp.zeros_like(st_ref)

    x = xbc_ref[...]
    ext_ref[0:SUBLANES, :] = carry_ref[...]
    ext_ref[SUBLANES:SUBLANES + CHUNK, :] = x
    carry_ref[...] = x[CHUNK - SUBLANES:CHUNK, :]
    cw = cw_ref[...]
    conv = cb_ref[...] + cw[SSD_CONV - 1:SSD_CONV] * x
    for k in range(SSD_CONV - 1):
        off = SUBLANES - (SSD_CONV - 1) + k
        conv = conv + cw[k:k + 1] * ext_ref[off:off + CHUNK, :]
    xa = _silu(conv)

    row = lax.broadcasted_iota(jnp.int32, (CHUNK, 1), 0)
    dt = _softplus(dt_ref[...] + dtb_ref[...])
    dt = jnp.where(jnp.logical_or(c > 0, row >= PAD), dt, 0.0)
    d = dt * (-jnp.exp(alog_ref[...]))
    li = lax.broadcasted_iota(jnp.int32, (CHUNK, CHUNK), 0)
    si = lax.broadcasted_iota(jnp.int32, (CHUNK, CHUNK), 1)
    causal = li >= si
    acum = _dot_sel_lhs(causal.astype(BF16), d)
    acum_t = _dot_sel_rhs(d.T, (li <= si).astype(BF16))

    for g in range(SSD_GROUPS):
        bg = xa[:, SSD_INNER + g * SSD_STATE:SSD_INNER + (g + 1) * SSD_STATE]
        cg = xa[:, SSD_INNER + (SSD_GROUPS + g) * SSD_STATE:SSD_INNER + (SSD_GROUPS + g + 1) * SSD_STATE]
        cb = _dot_nt(cg, bg)
        bg_t = bg.T
        for j in range(SSD_HPG):
            hd = g * SSD_HPG + j
            sl = slice(hd * SSD_HEADDIM, (hd + 1) * SSD_HEADDIM)
            xs = xa[:, sl]
            ac = acum[:, hd:hd + 1]
            seg = jnp.exp(jnp.minimum(ac - acum_t[hd:hd + 1, :], 0.0))
            m = jnp.where(causal, cb * seg, 0.0)
            xdt = xs * dt[:, hd:hd + 1]
            st = st_ref[hd]
            y = _dot(m, xdt) + _dot(cg, st) * jnp.exp(ac) + xs * dsk_ref[:, sl]
            y_ref[:, sl] = y
            a_last = acum[CHUNK - 1:CHUNK, hd:hd + 1]
            st_ref[hd] = st * jnp.exp(a_last) + _dot(bg_t, xdt * jnp.exp(a_last - ac))

    y = y_ref[...] * _silu(z_ref[...])
    gw = SSD_INNER // SSD_GROUPS
    for g in range(SSD_GROUPS):
        o_ref[:, g * gw:(g + 1) * gw] = _rms(y[:, g * gw:(g + 1) * gw], nw_ref[:, g * gw:(g + 1) * gw])


def _batch_chunk_spec(width):
    return pl.BlockSpec((CHUNK, width), lambda b, c: (c, b))


def _ssd(z, xbc, dtr, conv_w, conv_b, dt_bias, a_log, d_skip, norm_w, nb, lp):
    pad_h = lambda t: jnp.zeros((1, LANES), F32).at[0, :SSD_HEADS].set(t.astype(F32))
    cw8 = jnp.zeros((SUBLANES, SSD_XBC), F32).at[:SSD_CONV].set(conv_w.astype(F32))
    dsk = jnp.repeat(d_skip.astype(F32), SSD_HEADDIM).reshape(1, SSD_INNER)
    out = pl.pallas_call(
        _ssd_kernel,
        grid=(nb, lp // CHUNK),
        in_specs=[_batch_chunk_spec(SSD_INNER), _batch_chunk_spec(SSD_XBC), _batch_chunk_spec(LANES),
                  _const_spec((SUBLANES, SSD_XBC)), _const_spec((1, SSD_XBC)),
                  _const_spec((1, LANES)), _const_spec((1, LANES)),
                  _const_spec((1, SSD_INNER)), _const_spec((1, SSD_INNER))],
        out_specs=_batch_chunk_spec(SSD_INNER),
        out_shape=jax.ShapeDtypeStruct((lp, nb * SSD_INNER), F32),
        scratch_shapes=[pltpu.VMEM((SUBLANES, SSD_XBC), F32),
                        pltpu.VMEM((CHUNK + SUBLANES, SSD_XBC), F32),
                        pltpu.VMEM((SSD_HEADS, SSD_STATE, SSD_HEADDIM), F32),
                        pltpu.VMEM((CHUNK, SSD_INNER), F32)],
        compiler_params=_cparams("arbitrary", "arbitrary"),
        name="ssd",
    )(z.reshape(lp, nb * SSD_INNER), xbc.reshape(lp, nb * SSD_XBC), dtr.reshape(lp, nb * LANES),
      cw8, conv_b.reshape(1, SSD_XBC).astype(F32), pad_h(dt_bias), pad_h(a_log), dsk,
      norm_w.reshape(1, SSD_INNER).astype(F32))
    return out.reshape(lp * nb, SSD_INNER)


HG_LEVELS = 6


def _hgrn_tables():
    e = np.zeros((2 + HG_LEVELS, CHUNK, CHUNK), np.float32)
    m = np.zeros((1 + HG_LEVELS, CHUNK, CHUNK), np.float32)
    idx = np.arange(CHUNK)
    e[0] = idx[None, :] <= idx[:, None]
    e[1] = idx[None, :] > idx[:, None]
    m[0] = np.eye(CHUNK)
    for k in range(1, HG_LEVELS + 1):
        half = 1 << (k - 1)
        for r in range(CHUNK):
            bound = ((r >> k) << k) + half - 1
            if r > bound:
                e[1 + k, r, bound + 1:r + 1] = 1.0
            else:
                e[1 + k, r, r + 1:bound + 1] = 1.0
        same = (idx[:, None] >> k) == (idx[None, :] >> k)
        upper = ((idx[:, None] >> (k - 1)) & 1) == 1
        lower = ((idx[None, :] >> (k - 1)) & 1) == 0
        m[k] = same & upper & lower
    return e.reshape(-1, CHUNK), m


def _hgrn_kernel(p_ref, lb_ref, nw_ref, e_ref, m_ref, o_ref, st_ref):
    @pl.when(pl.program_id(1) == 0)
    def _():
        st_ref[...] = jnp.zeros_like(st_ref)

    hd = HGRN_HEADDIM
    for h in range(HGRN_HEADS):
        q = _silu(p_ref[:, h * hd:(h + 1) * hd])
        f = p_ref[:, HGRN_WIDTH + h * hd:HGRN_WIDTH + (h + 1) * hd]
        iv = p_ref[:, 2 * HGRN_WIDTH + h * hd:2 * HGRN_WIDTH + (h + 1) * hd]
        og = p_ref[:, 3 * HGRN_WIDTH + h * hd:3 * HGRN_WIDTH + (h + 1) * hd]
        lb = lb_ref[:, h * hd:(h + 1) * hd]
        forget = lb + (1.0 - lb) * _sigmoid(f)
        kf = 1.0 - forget
        eg = jnp.exp(_dot_sel_lhs(e_ref[...], jnp.log(forget)))
        att = m_ref[0] * _dot_nt(q, kf)
        for k in range(1, HG_LEVELS + 1):
            ek = eg[(1 + k) * CHUNK:(2 + k) * CHUNK]
            att = att + m_ref[k] * _dot_nt(q * ek, kf * ek)
        st = st_ref[h]
        out = _dot(att, iv) + _dot_nt(q * eg[0:CHUNK], st)
        st_ref[h] = st * eg[CHUNK - 1:CHUNK] + _dot_tn(iv, kf * eg[CHUNK:2 * CHUNK])
        out = _rms(out, nw_ref[:, h * hd:(h + 1) * hd])
        o_ref[:, h * hd:(h + 1) * hd] = out * _silu(og)


def _hgrn(p, lb, norm_w, nb, lp):
    e_np, m_np = _hgrn_tables()
    out = pl.pallas_call(
        _hgrn_kernel,
        grid=(nb, lp // CHUNK),
        in_specs=[_batch_chunk_spec(HGRN_IN), _const_spec((1, HGRN_WIDTH)), _const_spec((1, HGRN_WIDTH)),
                  _const_spec(e_np.shape), _const_spec(m_np.shape)],
        out_specs=_batch_chunk_spec(HGRN_WIDTH),
        out_shape=jax.ShapeDtypeStruct((lp, nb * HGRN_WIDTH), F32),
        scratch_shapes=[pltpu.VMEM((HGRN_HEADS, HGRN_HEADDIM, HGRN_HEADDIM), F32)],
        compiler_params=_cparams("arbitrary", "arbitrary"),
        name="hgrn2",
    )(p.reshape(lp, nb * HGRN_IN), lb.reshape(1, HGRN_WIDTH).astype(F32),
      norm_w.reshape(1, HGRN_WIDTH).astype(F32), jnp.asarray(e_np, BF16), jnp.asarray(m_np, F32))
    return out.reshape(lp * nb, HGRN_WIDTH)


def _head_ones():
    idx = np.arange(RWKV_WIDTH) // RWKV_HEADDIM
    return (idx[:, None] == idx[None, :]).astype(np.float32)


def _rwkv_prep_kernel(*refs, tm, nb, has_vres):
    if has_vres:
        (p_ref, mu_ref, w0_ref, w2_ref, a0_ref, a2_ref, g2_ref, kk_ref, ka_ref, rk_ref, ones_ref,
         vf_ref, v0_ref, v2_ref,
         r_o, w_o, k_o, v_o, kkn_o, b_o, g_o, bv_o, carry_ref, ext_ref) = refs
    else:
        (p_ref, mu_ref, w0_ref, w2_ref, a0_ref, a2_ref, g2_ref, kk_ref, ka_ref, rk_ref, ones_ref,
         r_o, w_o, k_o, v_o, kkn_o, b_o, g_o, bv_o, carry_ref, ext_ref) = refs

    @pl.when(pl.program_id(0) == 0)
    def _():
        carry_ref[...] = jnp.zeros_like(carry_ref)

    p = p_ref[...]
    ext_ref[0:nb, :] = carry_ref[...]
    ext_ref[nb:nb + tm, :] = p
    carry_ref[...] = p[tm - nb:tm, :]
    prev = ext_ref[0:tm, :]
    ps = p + (prev - p) * mu_ref[...]

    wd = RWKV_WIDTH
    r, k, v = ps[:, :wd], ps[:, wd:2 * wd], ps[:, 2 * wd:3 * wd]
    pwa = ps[:, 3 * wd:3 * wd + LORA_W + LORA_A]
    pg = ps[:, 3 * wd + LORA_W + LORA_A:RWKV_IN]
    w_log = -_softplus(-(w0_ref[...] + _dot(jnp.tanh(pwa), w2_ref[...]))) - 0.5
    w_o[...] = jnp.exp(-jnp.exp(w_log))
    a = _sigmoid(a0_ref[...] + _dot(pwa, a2_ref[...]))
    if has_vres:
        pv = ps[:, RWKV_IN:RWKV_INP]
        v = v + (vf_ref[...] - v) * _sigmoid(v0_ref[...] + _dot(pv, v2_ref[...]))
    g_o[...] = _dot(_sigmoid(pg), g2_ref[...])
    kk = k * kk_ref[...]
    ss = _dot_sel_rhs(kk * kk, ones_ref[...])
    kk = kk * lax.rsqrt(jnp.maximum(ss, 1e-24))
    k2 = k * (1.0 + (a - 1.0) * ka_ref[...])
    bonus = _dot_sel_rhs(r * k2 * rk_ref[...], ones_ref[...])
    r_o[...] = r
    k_o[...] = k2
    v_o[...] = v
    kkn_o[...] = kk
    b_o[...] = kk * a
    bv_o[...] = bonus * v


def _rwkv_prep(p, mu, w0, w2, a0, a2, g2, k_k, k_a, r_k, vres, tm, nb):
    rows = p.shape[0]
    wd = RWKV_WIDTH
    row = lambda t: t.reshape(1, -1).astype(F32)
    w2p = jnp.zeros((LORA_W + LORA_A, wd), F32).at[:LORA_W].set(w2).astype(BF16)
    a2p = jnp.zeros((LORA_W + LORA_A, wd), F32).at[LORA_W:].set(a2).astype(BF16)
    args = [p, row(mu), row(w0), w2p, row(a0), a2p, g2.astype(BF16), row(k_k), row(k_a), row(r_k),
            jnp.asarray(_head_ones(), BF16)]
    specs = [pl.BlockSpec((tm, RWKV_INP), lambda i: (i, 0)), _const_spec((1, RWKV_INP)),
             _const_spec((1, wd)), _const_spec((LORA_W + LORA_A, wd)), _const_spec((1, wd)),
             _const_spec((LORA_W + LORA_A, wd)), _const_spec((LORA_G, wd)), _const_spec((1, wd)),
             _const_spec((1, wd)), _const_spec((1, wd)), _const_spec((wd, wd))]
    if vres is not None:
        v_first, v0, v2 = vres
        v2p = jnp.zeros((RWKV_INP - RWKV_IN, wd), F32).at[:LORA_V].set(v2).astype(BF16)
        args += [v_first, row(v0), v2p]
        specs += [pl.BlockSpec((tm, wd), lambda i: (i, 0)), _const_spec((1, wd)),
                  _const_spec((RWKV_INP - RWKV_IN, wd))]
    return pl.pallas_call(
        functools.partial(_rwkv_prep_kernel, tm=tm, nb=nb, has_vres=vres is not None),
        grid=(rows // tm,),
        in_specs=specs,
        out_specs=[pl.BlockSpec((tm, wd), lambda i: (i, 0))] * 8,
        out_shape=[jax.ShapeDtypeStruct((rows, wd), F32)] * 8,
        scratch_shapes=[pltpu.VMEM((nb, RWKV_INP), F32),
                        pltpu.VMEM((tm + nb, RWKV_INP), F32)],
        compiler_params=_cparams("arbitrary"),
        name="rwkv_prep",
    )(*args)


RW_VLO = RWKV_HEADDIM // 2
RW_ACC = 4


def _rwkv_scan_kernel(r_ref, w_ref, k_ref, kk_ref, b_ref, v_ref, o_ref, m_ref, *, tb, nl):
    @pl.when(pl.program_id(0) == 0)
    def _():
        m_ref[...] = jnp.zeros_like(m_ref)

    shape = (RW_VLO, nl)

    def tree(parts):
        while len(parts) > 1:
            parts = [parts[i] + parts[i + 1] for i in range(0, len(parts), 2)]
        return parts[0]

    def step(t, carry):
        vt = v_ref[t]
        kk_t = kk_ref[t]
        acc = [None] * RW_ACC
        for k in range(RWKV_HEADDIM):
            term = m_ref[k] * jnp.broadcast_to(kk_t[k:k + 1], shape)
            acc[k % RW_ACC] = term if acc[k % RW_ACC] is None else acc[k % RW_ACC] + term
        sa = tree(acc)
        w_t, b_t, k_t, r_t = w_ref[t], b_ref[t], k_ref[t], r_ref[t]
        acc = [None] * RW_ACC
        for k in range(RWKV_HEADDIM):
            bc = lambda x: jnp.broadcast_to(x[k:k + 1], shape)
            mk = m_ref[k] * bc(w_t) - sa * bc(b_t) + vt * bc(k_t)
            m_ref[k] = mk
            term = mk * bc(r_t)
            acc[k % RW_ACC] = term if acc[k % RW_ACC] is None else acc[k % RW_ACC] + term
        o_ref[t] = tree(acc)
        return carry

    lax.fori_loop(0, tb, step, 0, unroll=2)


def _rwkv_scan(r, w, k2, kk, b, v, nb, lp, tb):
    nh, hd = RWKV_HEADS, RWKV_HEADDIM
    ni = nb * nh
    nl = 2 * ni

    def k_layout(t):
        t = t.reshape(lp, ni, hd).transpose(0, 2, 1)
        return jnp.concatenate([t, t], axis=-1)

    def v_layout(t):
        return t.reshape(lp, ni, 2, RW_VLO).transpose(0, 3, 2, 1).reshape(lp, RW_VLO, nl)

    kspec = pl.BlockSpec((tb, hd, nl), lambda i: (i, 0, 0))
    vspec = pl.BlockSpec((tb, RW_VLO, nl), lambda i: (i, 0, 0))
    o = pl.pallas_call(
        functools.partial(_rwkv_scan_kernel, tb=tb, nl=nl),
        grid=(lp // tb,),
        in_specs=[kspec] * 5 + [vspec],
        out_specs=vspec,
        out_shape=jax.ShapeDtypeStruct((lp, RW_VLO, nl), F32),
        scratch_shapes=[pltpu.VMEM((hd, RW_VLO, nl), F32)],
        compiler_params=_cparams("arbitrary"),
        name="rwkv_scan",
    )(k_layout(r), k_layout(w), k_layout(k2), k_layout(kk), k_layout(b), v_layout(v))
    o = o.reshape(lp, RW_VLO, 2, ni).transpose(0, 3, 2, 1)
    return o.reshape(lp * nb, RWKV_WIDTH)


def _rwkv_post_kernel(o_ref, bv_ref, g_ref, lw_ref, lb_ref, ones_ref, y_ref):
    o = o_ref[...]
    inv = 1.0 / RWKV_HEADDIM
    mean = _dot_sel_rhs(o, ones_ref[...]) * inv
    xc = o - mean
    var = _dot_sel_rhs(xc * xc, ones_ref[...]) * inv
    y = xc * lax.rsqrt(var + GN_EPS) * lw_ref[...] + lb_ref[...] + bv_ref[...]
    y_ref[...] = y * g_ref[...]


def _rwkv_post(o, bv, g, ln_w, ln_b, tm):
    rows = o.shape[0]
    wd = RWKV_WIDTH
    blk = pl.BlockSpec((tm, wd), lambda i: (i, 0))
    return pl.pallas_call(
        _rwkv_post_kernel,
        grid=(rows // tm,),
        in_specs=[blk, blk, blk, _const_spec((1, wd)), _const_spec((1, wd)), _const_spec((wd, wd))],
        out_specs=blk,
        out_shape=jax.ShapeDtypeStruct((rows, wd), F32),
        compiler_params=_cparams("arbitrary"),
        name="rwkv_post",
    )(o, bv, g, ln_w.reshape(1, wd).astype(F32), ln_b.reshape(1, wd).astype(F32),
      jnp.asarray(_head_ones(), BF16))


def _even_weight(w_in):
    padw = jnp.zeros((D_MODEL, LANES - SSD_HEADS), w_in.dtype)
    return jnp.concatenate([w_in, padw], axis=1).astype(BF16)


def kernel(x, meta, norm_mix_pre, norm_mix_post, norm_ffn_pre, norm_ffn_post, mix_w_out, ffn_w_up, ffn_conv_w, ffn_conv_b, ffn_w_down, ev_w_in, s5_lam_re, s5_lam_im, s5_log_dt, s5_b_re, s5_b_im, s5_c_re, s5_c_im, s5_d, s5_w_glu, s5_b_glu, ssd_conv_w, ssd_conv_b, ssd_dt_bias, ssd_a_log, ssd_d, ssd_norm, od_w_in, rw_mu, rw_w0, rw_w2, rw_a0, rw_a2, rw_g2, rw_k_k, rw_k_a, rw_r_k, rw_ln_w, rw_ln_b, rw_w_vin, rw_mu_v, rw_v0, rw_v2, hg_lb_raw, hg_norm):
    nb, seq, _ = x.shape
    depth = norm_mix_pre.shape[0]
    lp = PAD + N_META + seq
    rows = lp * nb
    tm = _row_tile(rows, 512)
    h = jnp.concatenate([jnp.zeros((PAD, nb, D_MODEL), x.dtype),
                         jnp.broadcast_to(meta.astype(x.dtype)[:, None], (N_META, nb, D_MODEL)),
                         x.transpose(1, 0, 2)], axis=0)
    h = h.reshape(rows, D_MODEL)

    lb_w = jax.nn.softmax(hg_lb_raw.astype(F32), axis=0)
    lb_table = jnp.cumsum(lb_w, axis=0) - lb_w[0]
    v_first = None
    s1 = S5_WIDTH
    s2 = s1 + SSD_INNER
    s3 = s2 + SSD_XBC
    for layer in range(depth):
        if layer % 2 == 0:
            e = layer // 2
            u, z, xbc, dtr = _norm_proj(h, norm_mix_pre[layer], _even_weight(ev_w_in[e]),
                                        ((0, s1), (s1, SSD_INNER), (s2, SSD_XBC), (s3, LANES)), tm)
            tables = _s5_tables(s5_lam_re[e], s5_lam_im[e], s5_log_dt[e], s5_b_re[e], s5_b_im[e],
                                s5_c_re[e], s5_c_im[e])
            y_a = _s5(u, tables, s5_d[e], s5_w_glu[e], s5_b_glu[e], nb, lp, CHUNK)
            y_b = _ssd(z, xbc, dtr, ssd_conv_w[e], ssd_conv_b[e], ssd_dt_bias[e], ssd_a_log[e], ssd_d[e],
                       ssd_norm[e], nb, lp)
        else:
            o = layer // 2
            mu = jnp.zeros((RWKV_INP,), F32).at[:RWKV_IN].set(rw_mu[o])
            w_rw = jnp.zeros((D_MODEL, RWKV_INP), F32).at[:, :RWKV_IN].set(od_w_in[o][:, :RWKV_IN])
            if o > 0:
                mu = mu.at[RWKV_IN:RWKV_IN + LORA_V].set(rw_mu_v[o - 1])
                w_rw = w_rw.at[:, RWKV_IN:RWKV_IN + LORA_V].set(rw_w_vin[o - 1])
            w_in = jnp.concatenate([w_rw, od_w_in[o][:, RWKV_IN:]], axis=1).astype(BF16)
            p_rw, p_hg = _norm_proj(h, norm_mix_pre[layer], w_in, ((0, RWKV_INP), (RWKV_INP, HGRN_IN)), tm)
            vres = None if o == 0 else (v_first, rw_v0[o - 1], rw_v2[o - 1])
            r, w, k2, v, kk, b, g, bv = _rwkv_prep(p_rw, mu, rw_w0[o], rw_w2[o], rw_a0[o], rw_a2[o],
                                                   rw_g2[o], rw_k_k[o], rw_k_a[o], rw_r_k[o], vres, tm, nb)
            if o == 0:
                v_first = v
            o_rw = _rwkv_scan(r, w, k2, kk, b, v, nb, lp, 16)
            y_a = _rwkv_post(o_rw, bv, g, rw_ln_w[o], rw_ln_b[o], tm)
            y_b = _hgrn(p_hg, lb_table[o], hg_norm[o], nb, lp)
        ka = y_a.shape[1]
        w_out = mix_w_out[layer].astype(BF16)
        h = _out_proj(y_a, y_b, w_out[:ka], w_out[ka:], norm_mix_post[layer], h, tm, nb)
        h = _ffn(h, norm_ffn_pre[layer], ffn_w_up[layer].astype(BF16), ffn_conv_w[layer], ffn_conv_b[layer],
                 ffn_w_down[layer].astype(BF16), norm_ffn_post[layer], tm, nb)
    return h.reshape(lp, nb, D_MODEL)[PAD + N_META:].transpose(1, 0, 2)
```

```python
import functools

import numpy as np
import jax
import jax.numpy as jnp
from jax import lax
from jax.experimental import pallas as pl
from jax.experimental.pallas import tpu as pltpu

F32 = jnp.float32
BF16 = jnp.bfloat16

D_MODEL = 1024
N_META = 16
CHUNK = 64
PAD = CHUNK - N_META
RMS_EPS = 1e-6

S5_WIDTH = 256
S5_GROUP = 16
S5_GROUPS = S5_WIDTH // S5_GROUP
S5_STATE = 64
S5_NS = S5_GROUPS * S5_STATE

SSD_HEADDIM = 64
SSD_INNER = 768
SSD_HEADS = SSD_INNER // SSD_HEADDIM
SSD_GROUPS = 2
SSD_HPG = SSD_HEADS // SSD_GROUPS
SSD_STATE = 128
SSD_CONV = 4
SSD_XBC = SSD_INNER + 2 * SSD_GROUPS * SSD_STATE
MIX_WIDTH = S5_WIDTH + SSD_INNER

RWKV_WIDTH = 512
RWKV_HEADDIM = 64
RWKV_HEADS = RWKV_WIDTH // RWKV_HEADDIM
LORA_W = 64
LORA_A = 64
LORA_V = 32
LORA_G = 128
GN_EPS = 64e-5
RWKV_IN = 3 * RWKV_WIDTH + LORA_W + LORA_A + LORA_G
RWKV_INP = RWKV_IN + 128

HGRN_WIDTH = 512
HGRN_HEADS = 4
HGRN_HEADDIM = HGRN_WIDTH // HGRN_HEADS
HGRN_IN = 4 * HGRN_WIDTH

D_FF = 2816
FFN_CONV = 3

LANES = 128
SUBLANES = 8
VMEM_LIMIT = 56 * 1024 * 1024


def _cparams(*sem):
    return pltpu.CompilerParams(dimension_semantics=sem, vmem_limit_bytes=VMEM_LIMIT)


def _row_tile(rows, target):
    best = SUBLANES
    for t in range(SUBLANES, min(rows, target) + 1, SUBLANES):
        if rows % t == 0:
            best = t
    return best


def _const_spec(shape):
    nd = len(shape)
    return pl.BlockSpec(shape, lambda *_: (0,) * nd, pipeline_mode=pl.Buffered(1))


def _dot(a, b):
    return jnp.dot(a.astype(BF16), b.astype(BF16), preferred_element_type=F32)


def _dot_nt(a, b):
    return lax.dot_general(a.astype(BF16), b.astype(BF16), (((1,), (1,)), ((), ())),
                           preferred_element_type=F32)


def _dot_tn(a, b):
    return lax.dot_general(a.astype(BF16), b.astype(BF16), (((0,), (0,)), ((), ())),
                           preferred_element_type=F32)


def _split3(x):
    hi = x.astype(BF16)
    r1 = x - hi.astype(F32)
    mid = r1.astype(BF16)
    lo = (r1 - mid.astype(F32)).astype(BF16)
    return hi, mid, lo


def _dot_sel_lhs(sel, x):
    hi, mid, lo = _split3(x)
    d = lambda p: jnp.dot(sel, p, preferred_element_type=F32)
    return d(hi) + d(mid) + d(lo)


def _dot_sel_rhs(x, sel):
    hi, mid, lo = _split3(x)
    d = lambda p: jnp.dot(p, sel, preferred_element_type=F32)
    return d(hi) + d(mid) + d(lo)


def _sigmoid(x):
    return 1.0 / (1.0 + jnp.exp(-x))


def _silu(x):
    return x * _sigmoid(x)


def _softplus(x):
    return jnp.maximum(x, 0.0) + jnp.log(1.0 + jnp.exp(-jnp.abs(x)))


GELU_C = 0.7978845608028654
GELU_A = 0.044715


def _gelu_tanh(x):
    return 0.5 * x * (1.0 + jnp.tanh(GELU_C * (x + GELU_A * (x * x * x))))


def _rms(x, g):
    return x * lax.rsqrt(jnp.mean(x * x, axis=-1, keepdims=True) + RMS_EPS) * g


def _keep_rows(tm, nb):
    row = pl.program_id(0) * tm + lax.broadcasted_iota(jnp.int32, (tm, 1), 0)
    return row >= PAD * nb


def _norm_proj_kernel(x_ref, g_ref, w_ref, *o_refs, splits, chunk):
    hn = _rms(x_ref[...], g_ref[...]).astype(BF16)
    for o_ref, (start, width) in zip(o_refs, splits):
        for c0 in range(0, width, chunk):
            cw = min(chunk, width - c0)
            o_ref[:, c0:c0 + cw] = jnp.dot(hn, w_ref[:, start + c0:start + c0 + cw],
                                           preferred_element_type=F32)


def _norm_proj(h, g, w, splits, tm):
    rows = h.shape[0]
    n = w.shape[1]
    return pl.pallas_call(
        functools.partial(_norm_proj_kernel, splits=splits, chunk=512),
        grid=(rows // tm,),
        in_specs=[pl.BlockSpec((tm, D_MODEL), lambda i: (i, 0)),
                  _const_spec((1, D_MODEL)),
                  _const_spec((D_MODEL, n))],
        out_specs=[pl.BlockSpec((tm, wd), lambda i: (i, 0)) for _, wd in splits],
        out_shape=[jax.ShapeDtypeStruct((rows, wd), F32) for _, wd in splits],
        compiler_params=_cparams("arbitrary"),
        name="norm_proj",
    )(h, g.reshape(1, D_MODEL), w)


def _out_proj_kernel(ya_ref, yb_ref, wa_ref, wb_ref, g_ref, h_ref, o_ref, *, tm, nb):
    o = _dot(ya_ref[...], wa_ref[...]) + _dot(yb_ref[...], wb_ref[...])
    upd = _rms(o, g_ref[...])
    o_ref[...] = h_ref[...] + jnp.where(_keep_rows(tm, nb), upd, 0.0)


def _out_proj(ya, yb, wa, wb, g, h, tm, nb):
    rows = h.shape[0]
    ka, kb = ya.shape[1], yb.shape[1]
    return pl.pallas_call(
        functools.partial(_out_proj_kernel, tm=tm, nb=nb),
        grid=(rows // tm,),
        in_specs=[pl.BlockSpec((tm, ka), lambda i: (i, 0)),
                  pl.BlockSpec((tm, kb), lambda i: (i, 0)),
                  _const_spec((ka, D_MODEL)),
                  _const_spec((kb, D_MODEL)),
                  _const_spec((1, D_MODEL)),
                  pl.BlockSpec((tm, D_MODEL), lambda i: (i, 0))],
        out_specs=pl.BlockSpec((tm, D_MODEL), lambda i: (i, 0)),
        out_shape=jax.ShapeDtypeStruct((rows, D_MODEL), F32),
        compiler_params=_cparams("arbitrary"),
        name="out_proj",
    )(ya, yb, wa, wb, g.reshape(1, D_MODEL), h)


def _ffn_kernel(h_ref, g1_ref, wup_ref, cw_ref, cb_ref, wdn_ref, g2_ref, o_ref, carry_ref, act_ref,
                *, tm, tf, nb):
    halo = (FFN_CONV - 1) * nb

    @pl.when(pl.program_id(0) == 0)
    def _():
        carry_ref[...] = jnp.zeros_like(carry_ref)

    x = h_ref[...]
    hn = _rms(x, g1_ref[...]).astype(BF16)
    for c in range(D_FF // tf):
        halves = []
        for part in range(2):
            col = part * D_FF + c * tf
            u = jnp.dot(hn, wup_ref[:, col:col + tf], preferred_element_type=F32)
            hist = carry_ref[:, col:col + tf]
            carry_ref[:, col:col + tf] = u[tm - halo:tm, :]
            prev2 = jnp.concatenate([hist, u[:tm - halo, :]], axis=0)
            prev1 = jnp.concatenate([hist[nb:, :], u[:tm - nb, :]], axis=0)
            cw = cw_ref[:, col:col + tf]
            halves.append(cw[0:1] * prev2 + cw[1:2] * prev1 + cw[2:3] * u + cb_ref[:, col:col + tf])
        gate, val = halves
        th = jnp.tanh(gate * (GELU_C + (GELU_C * GELU_A) * (gate * gate)))
        act_ref[:, c * tf:(c + 1) * tf] = (gate * (0.5 + 0.5 * th) * val).astype(BF16)
    acc = jnp.dot(act_ref[...], wdn_ref[...], preferred_element_type=F32)
    upd = _rms(acc, g2_ref[...])
    o_ref[...] = x + jnp.where(_keep_rows(tm, nb), upd, 0.0)


def _ffn(h, g1, wup, cw, cb, wdn, g2, tm, nb):
    rows = h.shape[0]
    tf = 256
    halo = (FFN_CONV - 1) * nb
    cw8 = jnp.zeros((SUBLANES, 2 * D_FF), F32).at[:FFN_CONV].set(cw)
    return pl.pallas_call(
        functools.partial(_ffn_kernel, tm=tm, tf=tf, nb=nb),
        grid=(rows // tm,),
        in_specs=[pl.BlockSpec((tm, D_MODEL), lambda i: (i, 0)),
                  _const_spec((1, D_MODEL)),
                  _const_spec((D_MODEL, 2 * D_FF)),
                  _const_spec((SUBLANES, 2 * D_FF)),
                  _const_spec((1, 2 * D_FF)),
                  _const_spec((D_FF, D_MODEL)),
                  _const_spec((1, D_MODEL))],
        out_specs=pl.BlockSpec((tm, D_MODEL), lambda i: (i, 0)),
        out_shape=jax.ShapeDtypeStruct((rows, D_MODEL), F32),
        scratch_shapes=[pltpu.VMEM((halo, 2 * D_FF), F32),
                        pltpu.VMEM((tm, D_FF), BF16)],
        compiler_params=_cparams("arbitrary"),
        name="conv_ffn",
    )(h, g1.reshape(1, D_MODEL), wup, cw8, cb.reshape(1, 2 * D_FF), wdn, g2.reshape(1, D_MODEL))


def _s5_kernel(u_ref, bmat_ref, are_ref, aim_ref, cmat_ref, d_ref, wg_ref, bg_ref, o_ref,
               h_ref, bu_ref, hs_ref, *, nb, q):
    @pl.when(pl.program_id(0) == 0)
    def _():
        h_ref[...] = jnp.zeros_like(h_ref)

    u = u_ref[...]
    bu_ref[...] = _dot(u, bmat_ref[...])
    a_re = jnp.broadcast_to(are_ref[...], (nb, S5_NS))
    a_im = jnp.broadcast_to(aim_ref[...], (nb, S5_NS))

    def step(t, carry):
        h_re, h_im = carry
        rows = pl.ds(pl.multiple_of(t * nb, nb), nb)
        n_re = a_re * h_re - a_im * h_im + bu_ref[rows, :S5_NS]
        n_im = a_re * h_im + a_im * h_re + bu_ref[rows, S5_NS:]
        hs_ref[rows, :S5_NS] = n_re
        hs_ref[rows, S5_NS:] = n_im
        return n_re, n_im

    h_re, h_im = lax.fori_loop(0, q, step, (h_ref[:, :S5_NS], h_ref[:, S5_NS:]), unroll=4)
    h_ref[:, :S5_NS] = h_re
    h_ref[:, S5_NS:] = h_im
    y = _gelu_tanh(_dot(hs_ref[...], cmat_ref[...]) + d_ref[...] * u)
    o_ref[...] = y * _sigmoid(_dot(y, wg_ref[...]) + bg_ref[...])


def _s5_tables(lam_re, lam_im, log_dt, b_re, b_im, c_re, c_im):
    lr, li = lam_re.astype(F32), lam_im.astype(F32)
    dt = jnp.exp(log_dt.astype(F32))[:, None]
    mag = jnp.exp(lr * dt)
    ab_re, ab_im = mag * jnp.cos(li * dt), mag * jnp.sin(li * dt)
    den = lr * lr + li * li
    zr, zi = ab_re - 1.0, ab_im
    f_re = (zr * lr + zi * li) / den
    f_im = (zi * lr - zr * li) / den
    br, bi = b_re.astype(F32), b_im.astype(F32)
    bb_re = f_re[..., None] * br - f_im[..., None] * bi
    bb_im = f_re[..., None] * bi + f_im[..., None] * br
    eye = jnp.eye(S5_GROUPS, dtype=F32)
    to_b = lambda t: jnp.einsum("gpc,gh->gchp", t, eye).reshape(S5_WIDTH, S5_NS)
    bmat = jnp.concatenate([to_b(bb_re), to_b(bb_im)], axis=1)
    to_c = lambda t: jnp.einsum("gcp,gh->gphc", t.astype(F32), eye).reshape(S5_NS, S5_WIDTH)
    cmat = jnp.concatenate([to_c(c_re), -to_c(c_im)], axis=0)
    return bmat.astype(BF16), ab_re.reshape(1, S5_NS), ab_im.reshape(1, S5_NS), cmat.astype(BF16)


def _s5(u, tables, d_skip, w_glu, b_glu, nb, lp, q):
    bmat, a_re, a_im, cmat = tables
    blk = pl.BlockSpec((q * nb, S5_WIDTH), lambda c: (c, 0))
    return pl.pallas_call(
        functools.partial(_s5_kernel, nb=nb, q=q),
        grid=(lp // q,),
        in_specs=[blk,
                  _const_spec((S5_WIDTH, 2 * S5_NS)),
                  _const_spec((1, S5_NS)),
                  _const_spec((1, S5_NS)),
                  _const_spec((2 * S5_NS, S5_WIDTH)),
                  _const_spec((1, S5_WIDTH)),
                  _const_spec((S5_WIDTH, S5_WIDTH)),
                  _const_spec((1, S5_WIDTH))],
        out_specs=blk,
        out_shape=jax.ShapeDtypeStruct((lp * nb, S5_WIDTH), F32),
        scratch_shapes=[pltpu.VMEM((nb, 2 * S5_NS), F32),
                        pltpu.VMEM((nb * q, 2 * S5_NS), F32),
                        pltpu.VMEM((nb * q, 2 * S5_NS), F32)],
        compiler_params=_cparams("arbitrary"),
        name="s5",
    )(u, bmat, a_re, a_im, cmat, d_skip.reshape(1, S5_WIDTH).astype(F32), w_glu.astype(BF16),
      b_glu.reshape(1, S5_WIDTH).astype(F32))


def _ssd_kernel(z_ref, xbc_ref, dt_ref, cw_ref, cb_ref, dtb_ref, alog_ref, dsk_ref, nw_ref, o_ref,
                carry_ref, ext_ref, st_ref, y_ref):
    c = pl.program_id(1)

    @pl.when(c == 0)
    def _():
        carry_ref[...] = jnp.zeros_like(carry_ref)
        st_ref[...] = jnp.zeros_like(st_ref)

    x = xbc_ref[...]
    ext_ref[0:SUBLANES, :] = carry_ref[...]
    ext_ref[SUBLANES:SUBLANES + CHUNK, :] = x
    carry_ref[...] = x[CHUNK - SUBLANES:CHUNK, :]
    cw = cw_ref[...]
    conv = cb_ref[...] + cw[SSD_CONV - 1:SSD_CONV] * x
    for k in range(SSD_CONV - 1):
        off = SUBLANES - (SSD_CONV - 1) + k
        conv = conv + cw[k:k + 1] * ext_ref[off:off + CHUNK, :]
    xa = _silu(conv)

    row = lax.broadcasted_iota(jnp.int32, (CHUNK, 1), 0)
    dt = _softplus(dt_ref[...] + dtb_ref[...])
    dt = jnp.where(jnp.logical_or(c > 0, row >= PAD), dt, 0.0)
    d = dt * (-jnp.exp(alog_ref[...]))
    li = lax.broadcasted_iota(jnp.int32, (CHUNK, CHUNK), 0)
    si = lax.broadcasted_iota(jnp.int32, (CHUNK, CHUNK), 1)
    causal = li >= si
    acum = _dot_sel_lhs(causal.astype(BF16), d)
    acum_t = _dot_sel_rhs(d.T, (li <= si).astype(BF16))

    for g in range(SSD_GROUPS):
        bg = xa[:, SSD_INNER + g * SSD_STATE:SSD_INNER + (g + 1) * SSD_STATE]
        cg = xa[:, SSD_INNER + (SSD_GROUPS + g) * SSD_STATE:SSD_INNER + (SSD_GROUPS + g + 1) * SSD_STATE]
        cb = _dot_nt(cg, bg)
        bg_t = bg.T
        for j in range(SSD_HPG):
            hd = g * SSD_HPG + j
            sl = slice(hd * SSD_HEADDIM, (hd + 1) * SSD_HEADDIM)
            xs = xa[:, sl]
            ac = acum[:, hd:hd + 1]
            seg = jnp.exp(jnp.minimum(ac - acum_t[hd:hd + 1, :], 0.0))
            m = jnp.where(causal, cb * seg, 0.0)
            xdt = xs * dt[:, hd:hd + 1]
            st = st_ref[hd]
            y = _dot(m, xdt) + _dot(cg, st) * jnp.exp(ac) + xs * dsk_ref[:, sl]
            y_ref[:, sl] = y
            a_last = acum[CHUNK - 1:CHUNK, hd:hd + 1]
            st_ref[hd] = st * jnp.exp(a_last) + _dot(bg_t, xdt * jnp.exp(a_last - ac))

    y = y_ref[...] * _silu(z_ref[...])
    gw = SSD_INNER // SSD_GROUPS
    for g in range(SSD_GROUPS):
        o_ref[:, g * gw:(g + 1) * gw] = _rms(y[:, g * gw:(g + 1) * gw], nw_ref[:, g * gw:(g + 1) * gw])


def _batch_chunk_spec(width):
    return pl.BlockSpec((CHUNK, width), lambda b, c: (c, b))


def _ssd(z, xbc, dtr, conv_w, conv_b, dt_bias, a_log, d_skip, norm_w, nb, lp):
    pad_h = lambda t: jnp.zeros((1, LANES), F32).at[0, :SSD_HEADS].set(t.astype(F32))
    cw8 = jnp.zeros((SUBLANES, SSD_XBC), F32).at[:SSD_CONV].set(conv_w.astype(F32))
    dsk = jnp.repeat(d_skip.astype(F32), SSD_HEADDIM).reshape(1, SSD_INNER)
    out = pl.pallas_call(
        _ssd_kernel,
        grid=(nb, lp // CHUNK),
        in_specs=[_batch_chunk_spec(SSD_INNER), _batch_chunk_spec(SSD_XBC), _batch_chunk_spec(LANES),
                  _const_spec((SUBLANES, SSD_XBC)), _const_spec((1, SSD_XBC)),
                  _const_spec((1, LANES)), _const_spec((1, LANES)),
                  _const_spec((1, SSD_INNER)), _const_spec((1, SSD_INNER))],
        out_specs=_batch_chunk_spec(SSD_INNER),
        out_shape=jax.ShapeDtypeStruct((lp, nb * SSD_INNER), F32),
        scratch_shapes=[pltpu.VMEM((SUBLANES, SSD_XBC), F32),
                        pltpu.VMEM((CHUNK + SUBLANES, SSD_XBC), F32),
                        pltpu.VMEM((SSD_HEADS, SSD_STATE, SSD_HEADDIM), F32),
                        pltpu.VMEM((CHUNK, SSD_INNER), F32)],
        compiler_params=_cparams("arbitrary", "arbitrary"),
        name="ssd",
    )(z.reshape(lp, nb * SSD_INNER), xbc.reshape(lp, nb * SSD_XBC), dtr.reshape(lp, nb * LANES),
      cw8, conv_b.reshape(1, SSD_XBC).astype(F32), pad_h(dt_bias), pad_h(a_log), dsk,
      norm_w.reshape(1, SSD_INNER).astype(F32))
    return out.reshape(lp * nb, SSD_INNER)


HG_LEVELS = 6


def _hgrn_tables():
    e = np.zeros((2 + HG_LEVELS, CHUNK, CHUNK), np.float32)
    m = np.zeros((1 + HG_LEVELS, CHUNK, CHUNK), np.float32)
    idx = np.arange(CHUNK)
    e[0] = idx[None, :] <= idx[:, None]
    e[1] = idx[None, :] > idx[:, None]
    m[0] = np.eye(CHUNK)
    for k in range(1, HG_LEVELS + 1):
        half = 1 << (k - 1)
        for r in range(CHUNK):
            bound = ((r >> k) << k) + half - 1
            if r > bound:
                e[1 + k, r, bound + 1:r + 1] = 1.0
            else:
                e[1 + k, r, r + 1:bound + 1] = 1.0
        same = (idx[:, None] >> k) == (idx[None, :] >> k)
        upper = ((idx[:, None] >> (k - 1)) & 1) == 1
        lower = ((idx[None, :] >> (k - 1)) & 1) == 0
        m[k] = same & upper & lower
    return e.reshape(-1, CHUNK), m


def _hgrn_kernel(p_ref, lb_ref, nw_ref, e_ref, m_ref, o_ref, st_ref):
    @pl.when(pl.program_id(1) == 0)
    def _():
        st_ref[...] = jnp.zeros_like(st_ref)

    hd = HGRN_HEADDIM
    for h in range(HGRN_HEADS):
        q = _silu(p_ref[:, h * hd:(h + 1) * hd])
        f = p_ref[:, HGRN_WIDTH + h * hd:HGRN_WIDTH + (h + 1) * hd]
        iv = p_ref[:, 2 * HGRN_WIDTH + h * hd:2 * HGRN_WIDTH + (h + 1) * hd]
        og = p_ref[:, 3 * HGRN_WIDTH + h * hd:3 * HGRN_WIDTH + (h + 1) * hd]
        lb = lb_ref[:, h * hd:(h + 1) * hd]
        forget = lb + (1.0 - lb) * _sigmoid(f)
        kf = 1.0 - forget
        eg = jnp.exp(_dot_sel_lhs(e_ref[...], jnp.log(forget)))
        att = m_ref[0] * _dot_nt(q, kf)
        for k in range(1, HG_LEVELS + 1):
            ek = eg[(1 + k) * CHUNK:(2 + k) * CHUNK]
            att = att + m_ref[k] * _dot_nt(q * ek, kf * ek)
        st = st_ref[h]
        out = _dot(att, iv) + _dot_nt(q * eg[0:CHUNK], st)
        st_ref[h] = st * eg[CHUNK - 1:CHUNK] + _dot_tn(iv, kf * eg[CHUNK:2 * CHUNK])
        out = _rms(out, nw_ref[:, h * hd:(h + 1) * hd])
        o_ref[:, h * hd:(h + 1) * hd] = out * _silu(og)


def _hgrn(p, lb, norm_w, nb, lp):
    e_np, m_np = _hgrn_tables()
    out = pl.pallas_call(
        _hgrn_kernel,
        grid=(nb, lp // CHUNK),
        in_specs=[_batch_chunk_spec(HGRN_IN), _const_spec((1, HGRN_WIDTH)), _const_spec((1, HGRN_WIDTH)),
                  _const_spec(e_np.shape), _const_spec(m_np.shape)],
        out_specs=_batch_chunk_spec(HGRN_WIDTH),
        out_shape=jax.ShapeDtypeStruct((lp, nb * HGRN_WIDTH), F32),
        scratch_shapes=[pltpu.VMEM((HGRN_HEADS, HGRN_HEADDIM, HGRN_HEADDIM), F32)],
        compiler_params=_cparams("arbitrary", "arbitrary"),
        name="hgrn2",
    )(p.reshape(lp, nb * HGRN_IN), lb.reshape(1, HGRN_WIDTH).astype(F32),
      norm_w.reshape(1, HGRN_WIDTH).astype(F32), jnp.asarray(e_np, BF16), jnp.asarray(m_np, F32))
    return out.reshape(lp * nb, HGRN_WIDTH)


def _head_ones():
    idx = np.arange(RWKV_WIDTH) // RWKV_HEADDIM
    return (idx[:, None] == idx[None, :]).astype(np.float32)


def _rwkv_prep_kernel(*refs, tm, nb, has_vres):
    if has_vres:
        (p_ref, mu_ref, w0_ref, w2_ref, a0_ref, a2_ref, g2_ref, kk_ref, ka_ref, rk_ref, ones_ref,
         vf_ref, v0_ref, v2_ref,
         r_o, w_o, k_o, v_o, kkn_o, b_o, g_o, bv_o, carry_ref, ext_ref) = refs
    else:
        (p_ref, mu_ref, w0_ref, w2_ref, a0_ref, a2_ref, g2_ref, kk_ref, ka_ref, rk_ref, ones_ref,
         r_o, w_o, k_o, v_o, kkn_o, b_o, g_o, bv_o, carry_ref, ext_ref) = refs

    @pl.when(pl.program_id(0) == 0)
    def _():
        carry_ref[...] = jnp.zeros_like(carry_ref)

    p = p_ref[...]
    ext_ref[0:nb, :] = carry_ref[...]
    ext_ref[nb:nb + tm, :] = p
    carry_ref[...] = p[tm - nb:tm, :]
    prev = ext_ref[0:tm, :]
    ps = p + (prev - p) * mu_ref[...]

    wd = RWKV_WIDTH
    r, k, v = ps[:, :wd], ps[:, wd:2 * wd], ps[:, 2 * wd:3 * wd]
    pwa = ps[:, 3 * wd:3 * wd + LORA_W + LORA_A]
    pg = ps[:, 3 * wd + LORA_W + LORA_A:RWKV_IN]
    w_log = -_softplus(-(w0_ref[...] + _dot(jnp.tanh(pwa), w2_ref[...]))) - 0.5
    w_o[...] = jnp.exp(-jnp.exp(w_log))
    a = _sigmoid(a0_ref[...] + _dot(pwa, a2_ref[...]))
    if has_vres:
        pv = ps[:, RWKV_IN:RWKV_INP]
        v = v + (vf_ref[...] - v) * _sigmoid(v0_ref[...] + _dot(pv, v2_ref[...]))
    g_o[...] = _dot(_sigmoid(pg), g2_ref[...])
    kk = k * kk_ref[...]
    ss = _dot_sel_rhs(kk * kk, ones_ref[...])
    kk = kk * lax.rsqrt(jnp.maximum(ss, 1e-24))
    k2 = k * (1.0 + (a - 1.0) * ka_ref[...])
    bonus = _dot_sel_rhs(r * k2 * rk_ref[...], ones_ref[...])
    r_o[...] = r
    k_o[...] = k2
    v_o[...] = v
    kkn_o[...] = kk
    b_o[...] = kk * a
    bv_o[...] = bonus * v


def _rwkv_prep(p, mu, w0, w2, a0, a2, g2, k_k, k_a, r_k, vres, tm, nb):
    rows = p.shape[0]
    wd = RWKV_WIDTH
    row = lambda t: t.reshape(1, -1).astype(F32)
    w2p = jnp.zeros((LORA_W + LORA_A, wd), F32).at[:LORA_W].set(w2).astype(BF16)
    a2p = jnp.zeros((LORA_W + LORA_A, wd), F32).at[LORA_W:].set(a2).astype(BF16)
    args = [p, row(mu), row(w0), w2p, row(a0), a2p, g2.astype(BF16), row(k_k), row(k_a), row(r_k),
            jnp.asarray(_head_ones(), BF16)]
    specs = [pl.BlockSpec((tm, RWKV_INP), lambda i: (i, 0)), _const_spec((1, RWKV_INP)),
             _const_spec((1, wd)), _const_spec((LORA_W + LORA_A, wd)), _const_spec((1, wd)),
             _const_spec((LORA_W + LORA_A, wd)), _const_spec((LORA_G, wd)), _const_spec((1, wd)),
             _const_spec((1, wd)), _const_spec((1, wd)), _const_spec((wd, wd))]
    if vres is not None:
        v_first, v0, v2 = vres
        v2p = jnp.zeros((RWKV_INP - RWKV_IN, wd), F32).at[:LORA_V].set(v2).astype(BF16)
        args += [v_first, row(v0), v2p]
        specs += [pl.BlockSpec((tm, wd), lambda i: (i, 0)), _const_spec((1, wd)),
                  _const_spec((RWKV_INP - RWKV_IN, wd))]
    return pl.pallas_call(
        functools.partial(_rwkv_prep_kernel, tm=tm, nb=nb, has_vres=vres is not None),
        grid=(rows // tm,),
        in_specs=specs,
        out_specs=[pl.BlockSpec((tm, wd), lambda i: (i, 0))] * 8,
        out_shape=[jax.ShapeDtypeStruct((rows, wd), F32)] * 8,
        scratch_shapes=[pltpu.VMEM((nb, RWKV_INP), F32),
                        pltpu.VMEM((tm + nb, RWKV_INP), F32)],
        compiler_params=_cparams("arbitrary"),
        name="rwkv_prep",
    )(*args)


RW_VLO = RWKV_HEADDIM // 2
RW_ACC = 4


RW_IO_UNROLL = 4


def _rwkv_scan_kernel(r_ref, w_ref, k_ref, kk_ref, b_ref, v_ref, o_ref,
                      m_ref, rs_ref, ws_ref, ks_ref, kks_ref, bs_ref, vs_ref, os_ref, *, tb, ni):
    @pl.when(pl.program_id(0) == 0)
    def _():
        m_ref[...] = jnp.zeros_like(m_ref)

    nl = 2 * ni
    shape = (RW_VLO, nl)

    def rows_of(t):
        return pl.ds(pl.multiple_of(t * ni, ni), ni)

    def load(t, carry):
        for src, dst in ((r_ref, rs_ref), (w_ref, ws_ref), (k_ref, ks_ref), (kk_ref, kks_ref), (b_ref, bs_ref)):
            tt = src[rows_of(t), :].T
            dst[t] = jnp.concatenate([tt, tt], axis=1)
        tt = v_ref[rows_of(t), :].T
        vs_ref[t] = jnp.concatenate([tt[:RW_VLO], tt[RW_VLO:]], axis=1)
        return carry

    lax.fori_loop(0, tb, load, 0, unroll=RW_IO_UNROLL)

    def tree(parts):
        while len(parts) > 1:
            parts = [parts[i] + parts[i + 1] for i in range(0, len(parts), 2)]
        return parts[0]

    def step(t, carry):
        vt = vs_ref[t]
        kk_t = kks_ref[t]
        acc = [None] * RW_ACC
        for k in range(RWKV_HEADDIM):
            term = m_ref[k] * jnp.broadcast_to(kk_t[k:k + 1], shape)
            acc[k % RW_ACC] = term if acc[k % RW_ACC] is None else acc[k % RW_ACC] + term
        sa = tree(acc)
        w_t, b_t, k_t, r_t = ws_ref[t], bs_ref[t], ks_ref[t], rs_ref[t]
        acc = [None] * RW_ACC
        for k in range(RWKV_HEADDIM):
            bc = lambda x: jnp.broadcast_to(x[k:k + 1], shape)
            mk = m_ref[k] * bc(w_t) - sa * bc(b_t) + vt * bc(k_t)
            m_ref[k] = mk
            term = mk * bc(r_t)
            acc[k % RW_ACC] = term if acc[k % RW_ACC] is None else acc[k % RW_ACC] + term
        os_ref[t] = tree(acc)
        return carry

    lax.fori_loop(0, tb, step, 0, unroll=2)

    def store(t, carry):
        o = os_ref[t]
        o_ref[rows_of(t), :] = jnp.concatenate([o[:, :ni], o[:, ni:]], axis=0).T
        return carry

    lax.fori_loop(0, tb, store, 0, unroll=RW_IO_UNROLL)


def _rwkv_scan(r, w, k2, kk, b, v, nb, lp, tb):
    hd = RWKV_HEADDIM
    ni = nb * RWKV_HEADS
    nl = 2 * ni
    spec = pl.BlockSpec((tb * ni, hd), lambda i: (i, 0))
    pairs = lambda t: t.reshape(lp * ni, hd)
    o = pl.pallas_call(
        functools.partial(_rwkv_scan_kernel, tb=tb, ni=ni),
        grid=(lp // tb,),
        in_specs=[spec] * 6,
        out_specs=spec,
        out_shape=jax.ShapeDtypeStruct((lp * ni, hd), F32),
        scratch_shapes=[pltpu.VMEM((hd, RW_VLO, nl), F32)] + [pltpu.VMEM((tb, hd, nl), F32)] * 5
                       + [pltpu.VMEM((tb, RW_VLO, nl), F32)] * 2,
        compiler_params=_cparams("arbitrary"),
        name="rwkv_scan",
    )(pairs(r), pairs(w), pairs(k2), pairs(kk), pairs(b), pairs(v))
    return o.reshape(lp * nb, RWKV_WIDTH)


def _rwkv_post_kernel(o_ref, bv_ref, g_ref, lw_ref, lb_ref, ones_ref, y_ref):
    o = o_ref[...]
    inv = 1.0 / RWKV_HEADDIM
    mean = _dot_sel_rhs(o, ones_ref[...]) * inv
    xc = o - mean
    var = _dot_sel_rhs(xc * xc, ones_ref[...]) * inv
    y = xc * lax.rsqrt(var + GN_EPS) * lw_ref[...] + lb_ref[...] + bv_ref[...]
    y_ref[...] = y * g_ref[...]


def _rwkv_post(o, bv, g, ln_w, ln_b, tm):
    rows = o.shape[0]
    wd = RWKV_WIDTH
    blk = pl.BlockSpec((tm, wd), lambda i: (i, 0))
    return pl.pallas_call(
        _rwkv_post_kernel,
        grid=(rows // tm,),
        in_specs=[blk, blk, blk, _const_spec((1, wd)), _const_spec((1, wd)), _const_spec((wd, wd))],
        out_specs=blk,
        out_shape=jax.ShapeDtypeStruct((rows, wd), F32),
        compiler_params=_cparams("arbitrary"),
        name="rwkv_post",
    )(o, bv, g, ln_w.reshape(1, wd).astype(F32), ln_b.reshape(1, wd).astype(F32),
      jnp.asarray(_head_ones(), BF16))


def _even_weight(w_in):
    padw = jnp.zeros((D_MODEL, LANES - SSD_HEADS), w_in.dtype)
    return jnp.concatenate([w_in, padw], axis=1).astype(BF16)


def kernel(x, meta, norm_mix_pre, norm_mix_post, norm_ffn_pre, norm_ffn_post, mix_w_out, ffn_w_up, ffn_conv_w, ffn_conv_b, ffn_w_down, ev_w_in, s5_lam_re, s5_lam_im, s5_log_dt, s5_b_re, s5_b_im, s5_c_re, s5_c_im, s5_d, s5_w_glu, s5_b_glu, ssd_conv_w, ssd_conv_b, ssd_dt_bias, ssd_a_log, ssd_d, ssd_norm, od_w_in, rw_mu, rw_w0, rw_w2, rw_a0, rw_a2, rw_g2, rw_k_k, rw_k_a, rw_r_k, rw_ln_w, rw_ln_b, rw_w_vin, rw_mu_v, rw_v0, rw_v2, hg_lb_raw, hg_norm):
    nb, seq, _ = x.shape
    depth = norm_mix_pre.shape[0]
    lp = PAD + N_META + seq
    rows = lp * nb
    tm = _row_tile(rows, 512)
    h = jnp.concatenate([jnp.zeros((PAD, nb, D_MODEL), x.dtype),
                         jnp.broadcast_to(meta.astype(x.dtype)[:, None], (N_META, nb, D_MODEL)),
                         x.transpose(1, 0, 2)], axis=0)
    h = h.reshape(rows, D_MODEL)

    lb_w = jax.nn.softmax(hg_lb_raw.astype(F32), axis=0)
    lb_table = jnp.cumsum(lb_w, axis=0) - lb_w[0]
    v_first = None
    s1 = S5_WIDTH
    s2 = s1 + SSD_INNER
    s3 = s2 + SSD_XBC
    for layer in range(depth):
        if layer % 2 == 0:
            e = layer // 2
            u, z, xbc, dtr = _norm_proj(h, norm_mix_pre[layer], _even_weight(ev_w_in[e]),
                                        ((0, s1), (s1, SSD_INNER), (s2, SSD_XBC), (s3, LANES)), tm)
            tables = _s5_tables(s5_lam_re[e], s5_lam_im[e], s5_log_dt[e], s5_b_re[e], s5_b_im[e],
                                s5_c_re[e], s5_c_im[e])
            y_a = _s5(u, tables, s5_d[e], s5_w_glu[e], s5_b_glu[e], nb, lp, CHUNK)
            y_b = _ssd(z, xbc, dtr, ssd_conv_w[e], ssd_conv_b[e], ssd_dt_bias[e], ssd_a_log[e], ssd_d[e],
                       ssd_norm[e], nb, lp)
        else:
            o = layer // 2
            mu = jnp.zeros((RWKV_INP,), F32).at[:RWKV_IN].set(rw_mu[o])
            w_rw = jnp.zeros((D_MODEL, RWKV_INP), F32).at[:, :RWKV_IN].set(od_w_in[o][:, :RWKV_IN])
            if o > 0:
                mu = mu.at[RWKV_IN:RWKV_IN + LORA_V].set(rw_mu_v[o - 1])
                w_rw = w_rw.at[:, RWKV_IN:RWKV_IN + LORA_V].set(rw_w_vin[o - 1])
            w_in = jnp.concatenate([w_rw, od_w_in[o][:, RWKV_IN:]], axis=1).astype(BF16)
            p_rw, p_hg = _norm_proj(h, norm_mix_pre[layer], w_in, ((0, RWKV_INP), (RWKV_INP, HGRN_IN)), tm)
            vres = None if o == 0 else (v_first, rw_v0[o - 1], rw_v2[o - 1])
            r, w, k2, v, kk, b, g, bv = _rwkv_prep(p_rw, mu, rw_w0[o], rw_w2[o], rw_a0[o], rw_a2[o],
                                                   rw_g2[o], rw_k_k[o], rw_k_a[o], rw_r_k[o], vres, tm, nb)
            if o == 0:
                v_first = v
            o_rw = _rwkv_scan(r, w, k2, kk, b, v, nb, lp, 32)
            y_a = _rwkv_post(o_rw, bv, g, rw_ln_w[o], rw_ln_b[o], tm)
            y_b = _hgrn(p_hg, lb_table[o], hg_norm[o], nb, lp)
        ka = y_a.shape[1]
        w_out = mix_w_out[layer].astype(BF16)
        h = _out_proj(y_a, y_b, w_out[:ka], w_out[ka:], norm_mix_post[layer], h, tm, nb)
        h = _ffn(h, norm_ffn_pre[layer], ffn_w_up[layer].astype(BF16), ffn_conv_w[layer], ffn_conv_b[layer],
                 ffn_w_down[layer].astype(BF16), norm_ffn_post[layer], tm, nb)
    return h.reshape(lp, nb, D_MODEL)[PAD + N_META:].transpose(1, 0, 2)
```

```python
import functools

import numpy as np
import jax
import jax.numpy as jnp
from jax import lax
from jax.experimental import pallas as pl
from jax.experimental.pallas import tpu as pltpu

F32 = jnp.float32
BF16 = jnp.bfloat16

D_MODEL = 1024
N_META = 16
CHUNK = 64
PAD = CHUNK - N_META
RMS_EPS = 1e-6

S5_WIDTH = 256
S5_GROUP = 16
S5_GROUPS = S5_WIDTH // S5_GROUP
S5_STATE = 64
S5_NS = S5_GROUPS * S5_STATE

SSD_HEADDIM = 64
SSD_INNER = 768
SSD_HEADS = SSD_INNER // SSD_HEADDIM
SSD_GROUPS = 2
SSD_HPG = SSD_HEADS // SSD_GROUPS
SSD_STATE = 128
SSD_CONV = 4
SSD_XBC = SSD_INNER + 2 * SSD_GROUPS * SSD_STATE
MIX_WIDTH = S5_WIDTH + SSD_INNER

RWKV_WIDTH = 512
RWKV_HEADDIM = 64
RWKV_HEADS = RWKV_WIDTH // RWKV_HEADDIM
LORA_W = 64
LORA_A = 64
LORA_V = 32
LORA_G = 128
GN_EPS = 64e-5
RWKV_IN = 3 * RWKV_WIDTH + LORA_W + LORA_A + LORA_G
RWKV_INP = RWKV_IN + 128

HGRN_WIDTH = 512
HGRN_HEADS = 4
HGRN_HEADDIM = HGRN_WIDTH // HGRN_HEADS
HGRN_IN = 4 * HGRN_WIDTH

D_FF = 2816
FFN_CONV = 3

LANES = 128
SUBLANES = 8
VMEM_LIMIT = 56 * 1024 * 1024


def _cparams(*sem):
    return pltpu.CompilerParams(dimension_semantics=sem, vmem_limit_bytes=VMEM_LIMIT)


def _row_tile(rows, target):
    best = SUBLANES
    for t in range(SUBLANES, min(rows, target) + 1, SUBLANES):
        if rows % t == 0:
            best = t
    return best


def _const_spec(shape):
    nd = len(shape)
    return pl.BlockSpec(shape, lambda *_: (0,) * nd, pipeline_mode=pl.Buffered(1))


def _dot(a, b):
    return jnp.dot(a.astype(BF16), b.astype(BF16), preferred_element_type=F32)


def _dot_nt(a, b):
    return lax.dot_general(a.astype(BF16), b.astype(BF16), (((1,), (1,)), ((), ())),
                           preferred_element_type=F32)


def _dot_tn(a, b):
    return lax.dot_general(a.astype(BF16), b.astype(BF16), (((0,), (0,)), ((), ())),
                           preferred_element_type=F32)


def _split3(x):
    hi = x.astype(BF16)
    r1 = x - hi.astype(F32)
    mid = r1.astype(BF16)
    lo = (r1 - mid.astype(F32)).astype(BF16)
    return hi, mid, lo


def _dot_sel_lhs(sel, x):
    hi, mid, lo = _split3(x)
    d = lambda p: jnp.dot(sel, p, preferred_element_type=F32)
    return d(hi) + d(mid) + d(lo)


def _dot_sel_rhs(x, sel):
    hi, mid, lo = _split3(x)
    d = lambda p: jnp.dot(p, sel, preferred_element_type=F32)
    return d(hi) + d(mid) + d(lo)


def _sigmoid(x):
    return 1.0 / (1.0 + jnp.exp(-x))


def _silu(x):
    return x * _sigmoid(x)


def _softplus(x):
    return jnp.maximum(x, 0.0) + jnp.log(1.0 + jnp.exp(-jnp.abs(x)))


GELU_C = 0.7978845608028654
GELU_A = 0.044715


def _gelu_tanh(x):
    return 0.5 * x * (1.0 + jnp.tanh(GELU_C * (x + GELU_A * (x * x * x))))


def _rms(x, g):
    return x * lax.rsqrt(jnp.mean(x * x, axis=-1, keepdims=True) + RMS_EPS) * g


def _keep_rows(tm, nb):
    row = pl.program_id(0) * tm + lax.broadcasted_iota(jnp.int32, (tm, 1), 0)
    return row >= PAD * nb


def _norm_proj_kernel(x_ref, g_ref, w_ref, *refs, splits, chunk, nb):
    o_refs, tiles_ref = refs[:len(splits)], refs[len(splits)]
    tm = x_ref.shape[0]
    hn = _rms(x_ref[...], g_ref[...]).astype(BF16)
    for o_ref, (start, width, by_batch) in zip(o_refs, splits):
        for c0 in range(0, width, chunk):
            cw = min(chunk, width - c0)
            res = jnp.dot(hn, w_ref[:, start + c0:start + c0 + cw], preferred_element_type=F32)
            if not by_batch:
                o_ref[:, c0:c0 + cw] = res
            else:
                for j in range(cw // LANES):
                    tiles_ref[c0 // LANES + j] = res[:, j * LANES:(j + 1) * LANES]
        if by_batch:
            for j in range(width // LANES):
                for b in range(nb):
                    o_ref[:, b * width + j * LANES:b * width + (j + 1) * LANES] = (
                        tiles_ref[j, pl.ds(b, tm // nb, stride=nb), :])


def _norm_proj(h, g, w, splits, tm, nb):
    rows = h.shape[0]
    n = w.shape[1]
    spec = lambda wd, bb: pl.BlockSpec((tm // nb, nb * wd) if bb else (tm, wd), lambda i: (i, 0))
    shape = lambda wd, bb: jax.ShapeDtypeStruct((rows // nb, nb * wd) if bb else (rows, wd), F32)
    max_w = max([wd for _, wd, bb in splits if bb] + [LANES])
    return pl.pallas_call(
        functools.partial(_norm_proj_kernel, splits=splits, chunk=512, nb=nb),
        grid=(rows // tm,),
        in_specs=[pl.BlockSpec((tm, D_MODEL), lambda i: (i, 0)),
                  _const_spec((1, D_MODEL)),
                  _const_spec((D_MODEL, n))],
        out_specs=[spec(wd, bb) for _, wd, bb in splits],
        out_shape=[shape(wd, bb) for _, wd, bb in splits],
        scratch_shapes=[pltpu.VMEM((max_w // LANES, tm, LANES), F32)],
        compiler_params=_cparams("arbitrary"),
        name="norm_proj",
    )(h, g.reshape(1, D_MODEL), w)


def _out_proj_kernel(ya_ref, yb_ref, wa_ref, wb_ref, g_ref, h_ref, o_ref, tiles_ref, *, tm, nb, kb):
    for j in range(kb // LANES):
        for b in range(nb):
            tiles_ref[j, pl.ds(b, tm // nb, stride=nb), :] = yb_ref[:, b * kb + j * LANES:b * kb + (j + 1) * LANES]
    yb = jnp.concatenate([tiles_ref[j] for j in range(kb // LANES)], axis=1)
    o = _dot(ya_ref[...], wa_ref[...]) + _dot(yb, wb_ref[...])
    upd = _rms(o, g_ref[...])
    o_ref[...] = h_ref[...] + jnp.where(_keep_rows(tm, nb), upd, 0.0)


def _out_proj(ya, yb, wa, wb, g, h, tm, nb):
    rows = h.shape[0]
    ka, kb = ya.shape[1], yb.shape[1] // nb
    return pl.pallas_call(
        functools.partial(_out_proj_kernel, tm=tm, nb=nb, kb=kb),
        grid=(rows // tm,),
        in_specs=[pl.BlockSpec((tm, ka), lambda i: (i, 0)),
                  pl.BlockSpec((tm // nb, nb * kb), lambda i: (i, 0)),
                  _const_spec((ka, D_MODEL)),
                  _const_spec((kb, D_MODEL)),
                  _const_spec((1, D_MODEL)),
                  pl.BlockSpec((tm, D_MODEL), lambda i: (i, 0))],
        out_specs=pl.BlockSpec((tm, D_MODEL), lambda i: (i, 0)),
        out_shape=jax.ShapeDtypeStruct((rows, D_MODEL), F32),
        scratch_shapes=[pltpu.VMEM((kb // LANES, tm, LANES), F32)],
        compiler_params=_cparams("arbitrary"),
        name="out_proj",
    )(ya, yb, wa, wb, g.reshape(1, D_MODEL), h)


def _ffn_kernel(h_ref, g1_ref, wup_ref, cw_ref, cb_ref, wdn_ref, g2_ref, o_ref, carry_ref, act_ref,
                *, tm, tf, nb):
    halo = (FFN_CONV - 1) * nb

    @pl.when(pl.program_id(0) == 0)
    def _():
        carry_ref[...] = jnp.zeros_like(carry_ref)

    x = h_ref[...]
    hn = _rms(x, g1_ref[...]).astype(BF16)
    for c in range(D_FF // tf):
        halves = []
        for part in range(2):
            col = part * D_FF + c * tf
            u = jnp.dot(hn, wup_ref[:, col:col + tf], preferred_element_type=F32)
            hist = carry_ref[:, col:col + tf]
            carry_ref[:, col:col + tf] = u[tm - halo:tm, :]
            prev2 = jnp.concatenate([hist, u[:tm - halo, :]], axis=0)
            prev1 = jnp.concatenate([hist[nb:, :], u[:tm - nb, :]], axis=0)
            cw = cw_ref[:, col:col + tf]
            halves.append(cw[0:1] * prev2 + cw[1:2] * prev1 + cw[2:3] * u + cb_ref[:, col:col + tf])
        gate, val = halves
        th = jnp.tanh(gate * (GELU_C + (GELU_C * GELU_A) * (gate * gate)))
        act_ref[:, c * tf:(c + 1) * tf] = (gate * (0.5 + 0.5 * th) * val).astype(BF16)
    acc = jnp.dot(act_ref[...], wdn_ref[...], preferred_element_type=F32)
    upd = _rms(acc, g2_ref[...])
    o_ref[...] = x + jnp.where(_keep_rows(tm, nb), upd, 0.0)


def _ffn(h, g1, wup, cw, cb, wdn, g2, tm, nb):
    rows = h.shape[0]
    tf = 256
    halo = (FFN_CONV - 1) * nb
    cw8 = jnp.zeros((SUBLANES, 2 * D_FF), F32).at[:FFN_CONV].set(cw)
    return pl.pallas_call(
        functools.partial(_ffn_kernel, tm=tm, tf=tf, nb=nb),
        grid=(rows // tm,),
        in_specs=[pl.BlockSpec((tm, D_MODEL), lambda i: (i, 0)),
                  _const_spec((1, D_MODEL)),
                  _const_spec((D_MODEL, 2 * D_FF)),
                  _const_spec((SUBLANES, 2 * D_FF)),
                  _const_spec((1, 2 * D_FF)),
                  _const_spec((D_FF, D_MODEL)),
                  _const_spec((1, D_MODEL))],
        out_specs=pl.BlockSpec((tm, D_MODEL), lambda i: (i, 0)),
        out_shape=jax.ShapeDtypeStruct((rows, D_MODEL), F32),
        scratch_shapes=[pltpu.VMEM((halo, 2 * D_FF), F32),
                        pltpu.VMEM((tm, D_FF), BF16)],
        compiler_params=_cparams("arbitrary"),
        name="conv_ffn",
    )(h, g1.reshape(1, D_MODEL), wup, cw8, cb.reshape(1, 2 * D_FF), wdn, g2.reshape(1, D_MODEL))


def _s5_kernel(u_ref, bmat_ref, are_ref, aim_ref, cmat_ref, d_ref, wg_ref, bg_ref, o_ref,
               h_ref, bu_ref, hs_ref, *, nb, q):
    @pl.when(pl.program_id(0) == 0)
    def _():
        h_ref[...] = jnp.zeros_like(h_ref)

    u = u_ref[...]
    bu_ref[...] = _dot(u, bmat_ref[...])
    a_re = jnp.broadcast_to(are_ref[...], (nb, S5_NS))
    a_im = jnp.broadcast_to(aim_ref[...], (nb, S5_NS))

    def step(t, carry):
        h_re, h_im = carry
        rows = pl.ds(pl.multiple_of(t * nb, nb), nb)
        n_re = a_re * h_re - a_im * h_im + bu_ref[rows, :S5_NS]
        n_im = a_re * h_im + a_im * h_re + bu_ref[rows, S5_NS:]
        hs_ref[rows, :S5_NS] = n_re
        hs_ref[rows, S5_NS:] = n_im
        return n_re, n_im

    h_re, h_im = lax.fori_loop(0, q, step, (h_ref[:, :S5_NS], h_ref[:, S5_NS:]), unroll=4)
    h_ref[:, :S5_NS] = h_re
    h_ref[:, S5_NS:] = h_im
    y = _gelu_tanh(_dot(hs_ref[...], cmat_ref[...]) + d_ref[...] * u)
    o_ref[...] = y * _sigmoid(_dot(y, wg_ref[...]) + bg_ref[...])


def _s5_tables(lam_re, lam_im, log_dt, b_re, b_im, c_re, c_im):
    lr, li = lam_re.astype(F32), lam_im.astype(F32)
    dt = jnp.exp(log_dt.astype(F32))[:, None]
    mag = jnp.exp(lr * dt)
    ab_re, ab_im = mag * jnp.cos(li * dt), mag * jnp.sin(li * dt)
    den = lr * lr + li * li
    zr, zi = ab_re - 1.0, ab_im
    f_re = (zr * lr + zi * li) / den
    f_im = (zi * lr - zr * li) / den
    br, bi = b_re.astype(F32), b_im.astype(F32)
    bb_re = f_re[..., None] * br - f_im[..., None] * bi
    bb_im = f_re[..., None] * bi + f_im[..., None] * br
    eye = jnp.eye(S5_GROUPS, dtype=F32)
    to_b = lambda t: jnp.einsum("gpc,gh->gchp", t, eye).reshape(S5_WIDTH, S5_NS)
    bmat = jnp.concatenate([to_b(bb_re), to_b(bb_im)], axis=1)
    to_c = lambda t: jnp.einsum("gcp,gh->gphc", t.astype(F32), eye).reshape(S5_NS, S5_WIDTH)
    cmat = jnp.concatenate([to_c(c_re), -to_c(c_im)], axis=0)
    return bmat.astype(BF16), ab_re.reshape(1, S5_NS), ab_im.reshape(1, S5_NS), cmat.astype(BF16)


def _s5(u, tables, d_skip, w_glu, b_glu, nb, lp, q):
    bmat, a_re, a_im, cmat = tables
    blk = pl.BlockSpec((q * nb, S5_WIDTH), lambda c: (c, 0))
    return pl.pallas_call(
        functools.partial(_s5_kernel, nb=nb, q=q),
        grid=(lp // q,),
        in_specs=[blk,
                  _const_spec((S5_WIDTH, 2 * S5_NS)),
                  _const_spec((1, S5_NS)),
                  _const_spec((1, S5_NS)),
                  _const_spec((2 * S5_NS, S5_WIDTH)),
                  _const_spec((1, S5_WIDTH)),
                  _const_spec((S5_WIDTH, S5_WIDTH)),
                  _const_spec((1, S5_WIDTH))],
        out_specs=blk,
        out_shape=jax.ShapeDtypeStruct((lp * nb, S5_WIDTH), F32),
        scratch_shapes=[pltpu.VMEM((nb, 2 * S5_NS), F32),
                        pltpu.VMEM((nb * q, 2 * S5_NS), F32),
                        pltpu.VMEM((nb * q, 2 * S5_NS), F32)],
        compiler_params=_cparams("arbitrary"),
        name="s5",
    )(u, bmat, a_re, a_im, cmat, d_skip.reshape(1, S5_WIDTH).astype(F32), w_glu.astype(BF16),
      b_glu.reshape(1, S5_WIDTH).astype(F32))


def _ssd_kernel(z_ref, xbc_ref, dt_ref, cw_ref, cb_ref, dtb_ref, alog_ref, dsk_ref, nw_ref, o_ref,
                carry_ref, ext_ref, st_ref, y_ref):
    c = pl.program_id(1)

    @pl.when(c == 0)
    def _():
        carry_ref[...] = jnp.zeros_like(carry_ref)
        st_ref[...] = jnp.zeros_like(st_ref)

    x = xbc_ref[...]
    ext_ref[0:SUBLANES, :] = carry_ref[...]
    ext_ref[SUBLANES:SUBLANES + CHUNK, :] = x
    carry_ref[...] = x[CHUNK - SUBLANES:CHUNK, :]
    cw = cw_ref[...]
    conv = cb_ref[...] + cw[SSD_CONV - 1:SSD_CONV] * x
    for k in range(SSD_CONV - 1):
        off = SUBLANES - (SSD_CONV - 1) + k
        conv = conv + cw[k:k + 1] * ext_ref[off:off + CHUNK, :]
    xa = _silu(conv)

    row = lax.broadcasted_iota(jnp.int32, (CHUNK, 1), 0)
    dt = _softplus(dt_ref[...] + dtb_ref[...])
    dt = jnp.where(jnp.logical_or(c > 0, row >= PAD), dt, 0.0)
    d = dt * (-jnp.exp(alog_ref[...]))
    li = lax.broadcasted_iota(jnp.int32, (CHUNK, CHUNK), 0)
    si = lax.broadcasted_iota(jnp.int32, (CHUNK, CHUNK), 1)
    causal = li >= si
    acum = _dot_sel_lhs(causal.astype(BF16), d)
    acum_t = _dot_sel_rhs(d.T, (li <= si).astype(BF16))

    for g in range(SSD_GROUPS):
        bg = xa[:, SSD_INNER + g * SSD_STATE:SSD_INNER + (g + 1) * SSD_STATE]
        cg = xa[:, SSD_INNER + (SSD_GROUPS + g) * SSD_STATE:SSD_INNER + (SSD_GROUPS + g + 1) * SSD_STATE]
        cb = _dot_nt(cg, bg)
        bg_t = bg.T
        for j in range(SSD_HPG):
            hd = g * SSD_HPG + j
            sl = slice(hd * SSD_HEADDIM, (hd + 1) * SSD_HEADDIM)
            xs = xa[:, sl]
            ac = acum[:, hd:hd + 1]
            seg = jnp.exp(jnp.minimum(ac - acum_t[hd:hd + 1, :], 0.0))
            m = jnp.where(causal, cb * seg, 0.0)
            xdt = xs * dt[:, hd:hd + 1]
            st = st_ref[hd]
            y = _dot(m, xdt) + _dot(cg, st) * jnp.exp(ac) + xs * dsk_ref[:, sl]
            y_ref[:, sl] = y
            a_last = acum[CHUNK - 1:CHUNK, hd:hd + 1]
            st_ref[hd] = st * jnp.exp(a_last) + _dot(bg_t, xdt * jnp.exp(a_last - ac))

    y = y_ref[...] * _silu(z_ref[...])
    gw = SSD_INNER // SSD_GROUPS
    for g in range(SSD_GROUPS):
        o_ref[:, g * gw:(g + 1) * gw] = _rms(y[:, g * gw:(g + 1) * gw], nw_ref[:, g * gw:(g + 1) * gw])


def _batch_chunk_spec(width):
    return pl.BlockSpec((CHUNK, width), lambda b, c: (c, b))


def _ssd(z, xbc, dtr, conv_w, conv_b, dt_bias, a_log, d_skip, norm_w, nb, lp):
    pad_h = lambda t: jnp.zeros((1, LANES), F32).at[0, :SSD_HEADS].set(t.astype(F32))
    cw8 = jnp.zeros((SUBLANES, SSD_XBC), F32).at[:SSD_CONV].set(conv_w.astype(F32))
    dsk = jnp.repeat(d_skip.astype(F32), SSD_HEADDIM).reshape(1, SSD_INNER)
    out = pl.pallas_call(
        _ssd_kernel,
        grid=(nb, lp // CHUNK),
        in_specs=[_batch_chunk_spec(SSD_INNER), _batch_chunk_spec(SSD_XBC), _batch_chunk_spec(LANES),
                  _const_spec((SUBLANES, SSD_XBC)), _const_spec((1, SSD_XBC)),
                  _const_spec((1, LANES)), _const_spec((1, LANES)),
                  _const_spec((1, SSD_INNER)), _const_spec((1, SSD_INNER))],
        out_specs=_batch_chunk_spec(SSD_INNER),
        out_shape=jax.ShapeDtypeStruct((lp, nb * SSD_INNER), F32),
        scratch_shapes=[pltpu.VMEM((SUBLANES, SSD_XBC), F32),
                        pltpu.VMEM((CHUNK + SUBLANES, SSD_XBC), F32),
                        pltpu.VMEM((SSD_HEADS, SSD_STATE, SSD_HEADDIM), F32),
                        pltpu.VMEM((CHUNK, SSD_INNER), F32)],
        compiler_params=_cparams("arbitrary", "arbitrary"),
        name="ssd",
    )(z, xbc, dtr, cw8, conv_b.reshape(1, SSD_XBC).astype(F32), pad_h(dt_bias), pad_h(a_log), dsk,
      norm_w.reshape(1, SSD_INNER).astype(F32))
    return out


HG_LEVELS = 6


def _hgrn_tables():
    e = np.zeros((2 + HG_LEVELS, CHUNK, CHUNK), np.float32)
    m = np.zeros((1 + HG_LEVELS, CHUNK, CHUNK), np.float32)
    idx = np.arange(CHUNK)
    e[0] = idx[None, :] <= idx[:, None]
    e[1] = idx[None, :] > idx[:, None]
    m[0] = np.eye(CHUNK)
    for k in range(1, HG_LEVELS + 1):
        half = 1 << (k - 1)
        for r in range(CHUNK):
            bound = ((r >> k) << k) + half - 1
            if r > bound:
                e[1 + k, r, bound + 1:r + 1] = 1.0
            else:
                e[1 + k, r, r + 1:bound + 1] = 1.0
        same = (idx[:, None] >> k) == (idx[None, :] >> k)
        upper = ((idx[:, None] >> (k - 1)) & 1) == 1
        lower = ((idx[None, :] >> (k - 1)) & 1) == 0
        m[k] = same & upper & lower
    return e.reshape(-1, CHUNK), m


def _hgrn_kernel(p_ref, lb_ref, nw_ref, e_ref, m_ref, o_ref, st_ref):
    @pl.when(pl.program_id(1) == 0)
    def _():
        st_ref[...] = jnp.zeros_like(st_ref)

    hd = HGRN_HEADDIM
    for h in range(HGRN_HEADS):
        q = _silu(p_ref[:, h * hd:(h + 1) * hd])
        f = p_ref[:, HGRN_WIDTH + h * hd:HGRN_WIDTH + (h + 1) * hd]
        iv = p_ref[:, 2 * HGRN_WIDTH + h * hd:2 * HGRN_WIDTH + (h + 1) * hd]
        og = p_ref[:, 3 * HGRN_WIDTH + h * hd:3 * HGRN_WIDTH + (h + 1) * hd]
        lb = lb_ref[:, h * hd:(h + 1) * hd]
        forget = lb + (1.0 - lb) * _sigmoid(f)
        kf = 1.0 - forget
        eg = jnp.exp(_dot_sel_lhs(e_ref[...], jnp.log(forget)))
        att = m_ref[0] * _dot_nt(q, kf)
        for k in range(1, HG_LEVELS + 1):
            ek = eg[(1 + k) * CHUNK:(2 + k) * CHUNK]
            att = att + m_ref[k] * _dot_nt(q * ek, kf * ek)
        st = st_ref[h]
        out = _dot(att, iv) + _dot_nt(q * eg[0:CHUNK], st)
        st_ref[h] = st * eg[CHUNK - 1:CHUNK] + _dot_tn(iv, kf * eg[CHUNK:2 * CHUNK])
        out = _rms(out, nw_ref[:, h * hd:(h + 1) * hd])
        o_ref[:, h * hd:(h + 1) * hd] = out * _silu(og)


def _hgrn(p, lb, norm_w, nb, lp):
    e_np, m_np = _hgrn_tables()
    out = pl.pallas_call(
        _hgrn_kernel,
        grid=(nb, lp // CHUNK),
        in_specs=[_batch_chunk_spec(HGRN_IN), _const_spec((1, HGRN_WIDTH)), _const_spec((1, HGRN_WIDTH)),
                  _const_spec(e_np.shape), _const_spec(m_np.shape)],
        out_specs=_batch_chunk_spec(HGRN_WIDTH),
        out_shape=jax.ShapeDtypeStruct((lp, nb * HGRN_WIDTH), F32),
        scratch_shapes=[pltpu.VMEM((HGRN_HEADS, HGRN_HEADDIM, HGRN_HEADDIM), F32)],
        compiler_params=_cparams("arbitrary", "arbitrary"),
        name="hgrn2",
    )(p, lb.reshape(1, HGRN_WIDTH).astype(F32),
      norm_w.reshape(1, HGRN_WIDTH).astype(F32), jnp.asarray(e_np, BF16), jnp.asarray(m_np, F32))
    return out


def _head_ones():
    idx = np.arange(RWKV_WIDTH) // RWKV_HEADDIM
    return (idx[:, None] == idx[None, :]).astype(np.float32)


def _rwkv_prep_kernel(*refs, tm, nb, has_vres):
    (p_ref, mu_ref, w0_ref, w2_ref, a0_ref, a2_ref, g2_ref, kk_ref, ka_ref, rk_ref, ones_ref) = refs[:11]
    if has_vres:
        vf_ref, v0_ref, v2_ref = refs[11:14]
        kkb_o, wk_o, rv_o, g_o, bv_o, carry_ref, ext_ref = refs[14:]
    else:
        kkb_o, wk_o, rv_o, g_o, bv_o, v_o, carry_ref, ext_ref = refs[11:]

    @pl.when(pl.program_id(0) == 0)
    def _():
        carry_ref[...] = jnp.zeros_like(carry_ref)

    p = p_ref[...]
    ext_ref[0:nb, :] = carry_ref[...]
    ext_ref[nb:nb + tm, :] = p
    carry_ref[...] = p[tm - nb:tm, :]
    prev = ext_ref[0:tm, :]
    ps = p + (prev - p) * mu_ref[...]

    wd = RWKV_WIDTH
    r, k, v = ps[:, :wd], ps[:, wd:2 * wd], ps[:, 2 * wd:3 * wd]
    pwa = ps[:, 3 * wd:3 * wd + LORA_W + LORA_A]
    pg = ps[:, 3 * wd + LORA_W + LORA_A:RWKV_IN]
    w_log = -_softplus(-(w0_ref[...] + _dot(jnp.tanh(pwa), w2_ref[...]))) - 0.5
    w = jnp.exp(-jnp.exp(w_log))
    a = _sigmoid(a0_ref[...] + _dot(pwa, a2_ref[...]))
    if has_vres:
        pv = ps[:, RWKV_IN:RWKV_INP]
        v = v + (vf_ref[...] - v) * _sigmoid(v0_ref[...] + _dot(pv, v2_ref[...]))
    else:
        v_o[...] = v
    g_o[...] = _dot(_sigmoid(pg), g2_ref[...])
    kk = k * kk_ref[...]
    ss = _dot_sel_rhs(kk * kk, ones_ref[...])
    kk = kk * lax.rsqrt(jnp.maximum(ss, 1e-24))
    k2 = k * (1.0 + (a - 1.0) * ka_ref[...])
    bonus = _dot_sel_rhs(r * k2 * rk_ref[...], ones_ref[...])
    bv_o[...] = bonus * v
    hd = RWKV_HEADDIM
    for o_ref, first, second in ((kkb_o, kk, kk * a), (wk_o, w, k2), (rv_o, r, v)):
        for h in range(RWKV_HEADS):
            both = jnp.concatenate([first[:, h * hd:(h + 1) * hd], second[:, h * hd:(h + 1) * hd]], axis=1)
            o_ref[:, h] = both.reshape(tm // nb, nb, 2 * hd)


def _rwkv_prep(p, mu, w0, w2, a0, a2, g2, k_k, k_a, r_k, vres, tm, nb):
    rows = p.shape[0]
    wd = RWKV_WIDTH
    row = lambda t: t.reshape(1, -1).astype(F32)
    w2p = jnp.zeros((LORA_W + LORA_A, wd), F32).at[:LORA_W].set(w2).astype(BF16)
    a2p = jnp.zeros((LORA_W + LORA_A, wd), F32).at[LORA_W:].set(a2).astype(BF16)
    args = [p, row(mu), row(w0), w2p, row(a0), a2p, g2.astype(BF16), row(k_k), row(k_a), row(r_k),
            jnp.asarray(_head_ones(), BF16)]
    specs = [pl.BlockSpec((tm, RWKV_INP), lambda i: (i, 0)), _const_spec((1, RWKV_INP)),
             _const_spec((1, wd)), _const_spec((LORA_W + LORA_A, wd)), _const_spec((1, wd)),
             _const_spec((LORA_W + LORA_A, wd)), _const_spec((LORA_G, wd)), _const_spec((1, wd)),
             _const_spec((1, wd)), _const_spec((1, wd)), _const_spec((wd, wd))]
    if vres is not None:
        v_first, v0, v2 = vres
        v2p = jnp.zeros((RWKV_INP - RWKV_IN, wd), F32).at[:LORA_V].set(v2).astype(BF16)
        args += [v_first, row(v0), v2p]
        specs += [pl.BlockSpec((tm, wd), lambda i: (i, 0)), _const_spec((1, wd)),
                  _const_spec((RWKV_INP - RWKV_IN, wd))]
    pair_shape = (rows // nb, RWKV_HEADS, nb, 2 * RWKV_HEADDIM)
    pair_spec = pl.BlockSpec((tm // nb,) + pair_shape[1:], lambda i: (i, 0, 0, 0))
    row_spec = pl.BlockSpec((tm, wd), lambda i: (i, 0))
    row_shape = jax.ShapeDtypeStruct((rows, wd), F32)
    n_row = 2 if vres is not None else 3
    return pl.pallas_call(
        functools.partial(_rwkv_prep_kernel, tm=tm, nb=nb, has_vres=vres is not None),
        grid=(rows // tm,),
        in_specs=specs,
        out_specs=[pair_spec] * 3 + [row_spec] * n_row,
        out_shape=[jax.ShapeDtypeStruct(pair_shape, F32)] * 3 + [row_shape] * n_row,
        scratch_shapes=[pltpu.VMEM((nb, RWKV_INP), F32),
                        pltpu.VMEM((tm + nb, RWKV_INP), F32)],
        compiler_params=_cparams("arbitrary"),
        name="rwkv_prep",
    )(*args)


RW_VLO = RWKV_HEADDIM // 2
RW_ACC = 4


RW_IO_UNROLL = 4


def _rwkv_scan_kernel(kkb_ref, wk_ref, rv_ref, o_ref,
                      m_ref, rs_ref, ws_ref, ks_ref, kks_ref, bs_ref, vs_ref, os_ref, *, tb, ni):
    @pl.when(pl.program_id(0) == 0)
    def _():
        m_ref[...] = jnp.zeros_like(m_ref)

    hd = RWKV_HEADDIM
    nl = 2 * ni
    shape = (RW_VLO, nl)

    def load(t, carry):
        for src, first, second in ((kkb_ref, kks_ref, bs_ref), (wk_ref, ws_ref, ks_ref), (rv_ref, rs_ref, None)):
            tt = src[t].reshape(ni, 2 * hd).T
            first[t] = jnp.concatenate([tt[:hd], tt[:hd]], axis=1)
            if second is not None:
                second[t] = jnp.concatenate([tt[hd:], tt[hd:]], axis=1)
            else:
                vs_ref[t] = jnp.concatenate([tt[hd:hd + RW_VLO], tt[hd + RW_VLO:]], axis=1)
        return carry

    lax.fori_loop(0, tb, load, 0, unroll=RW_IO_UNROLL)

    def tree(parts):
        while len(parts) > 1:
            parts = [parts[i] + parts[i + 1] for i in range(0, len(parts), 2)]
        return parts[0]

    def step(t, carry):
        vt = vs_ref[t]
        kk_t = kks_ref[t]
        acc = [None] * RW_ACC
        for k in range(hd):
            term = m_ref[k] * jnp.broadcast_to(kk_t[k:k + 1], shape)
            acc[k % RW_ACC] = term if acc[k % RW_ACC] is None else acc[k % RW_ACC] + term
        sa = tree(acc)
        w_t, b_t, k_t, r_t = ws_ref[t], bs_ref[t], ks_ref[t], rs_ref[t]
        acc = [None] * RW_ACC
        for k in range(hd):
            bc = lambda x: jnp.broadcast_to(x[k:k + 1], shape)
            mk = m_ref[k] * bc(w_t) - sa * bc(b_t) + vt * bc(k_t)
            m_ref[k] = mk
            term = mk * bc(r_t)
            acc[k % RW_ACC] = term if acc[k % RW_ACC] is None else acc[k % RW_ACC] + term
        os_ref[t] = tree(acc)
        return carry

    lax.fori_loop(0, tb, step, 0, unroll=2)

    def store(t, carry):
        o = os_ref[t]
        o_ref[t] = jnp.concatenate([o[:, :ni], o[:, ni:]], axis=0).T.reshape(o_ref.shape[1:])
        return carry

    lax.fori_loop(0, tb, store, 0, unroll=RW_IO_UNROLL)


def _rwkv_scan(kkb, wk, rv, nb, lp, tb):
    nh, hd = RWKV_HEADS, RWKV_HEADDIM
    ni = nb * nh
    nl = 2 * ni
    in_spec = pl.BlockSpec((tb, nh, nb, 2 * hd), lambda i: (i, 0, 0, 0))
    return pl.pallas_call(
        functools.partial(_rwkv_scan_kernel, tb=tb, ni=ni),
        grid=(lp // tb,),
        in_specs=[in_spec] * 3,
        out_specs=pl.BlockSpec((tb, nh, nb, hd), lambda i: (i, 0, 0, 0)),
        out_shape=jax.ShapeDtypeStruct((lp, nh, nb, hd), F32),
        scratch_shapes=[pltpu.VMEM((hd, RW_VLO, nl), F32)] + [pltpu.VMEM((tb, hd, nl), F32)] * 5
                       + [pltpu.VMEM((tb, RW_VLO, nl), F32)] * 2,
        compiler_params=_cparams("arbitrary"),
        name="rwkv_scan",
    )(kkb, wk, rv)


def _rwkv_post_kernel(o_ref, bv_ref, g_ref, lw_ref, lb_ref, ones_ref, y_ref, *, tm):
    o = jnp.concatenate([o_ref[:, h].reshape(tm, RWKV_HEADDIM) for h in range(RWKV_HEADS)], axis=1)
    inv = 1.0 / RWKV_HEADDIM
    mean = _dot_sel_rhs(o, ones_ref[...]) * inv
    xc = o - mean
    var = _dot_sel_rhs(xc * xc, ones_ref[...]) * inv
    y = xc * lax.rsqrt(var + GN_EPS) * lw_ref[...] + lb_ref[...] + bv_ref[...]
    y_ref[...] = y * g_ref[...]


def _rwkv_post(o, bv, g, ln_w, ln_b, tm, nb):
    rows = bv.shape[0]
    wd = RWKV_WIDTH
    blk = pl.BlockSpec((tm, wd), lambda i: (i, 0))
    return pl.pallas_call(
        functools.partial(_rwkv_post_kernel, tm=tm),
        grid=(rows // tm,),
        in_specs=[pl.BlockSpec((tm // nb, RWKV_HEADS, nb, RWKV_HEADDIM), lambda i: (i, 0, 0, 0)),
                  blk, blk, _const_spec((1, wd)), _const_spec((1, wd)), _const_spec((wd, wd))],
        out_specs=blk,
        out_shape=jax.ShapeDtypeStruct((rows, wd), F32),
        compiler_params=_cparams("arbitrary"),
        name="rwkv_post",
    )(o, bv, g, ln_w.reshape(1, wd).astype(F32), ln_b.reshape(1, wd).astype(F32),
      jnp.asarray(_head_ones(), BF16))


def _even_weight(w_in):
    padw = jnp.zeros((D_MODEL, LANES - SSD_HEADS), w_in.dtype)
    return jnp.concatenate([w_in, padw], axis=1).astype(BF16)


def kernel(x, meta, norm_mix_pre, norm_mix_post, norm_ffn_pre, norm_ffn_post, mix_w_out, ffn_w_up, ffn_conv_w, ffn_conv_b, ffn_w_down, ev_w_in, s5_lam_re, s5_lam_im, s5_log_dt, s5_b_re, s5_b_im, s5_c_re, s5_c_im, s5_d, s5_w_glu, s5_b_glu, ssd_conv_w, ssd_conv_b, ssd_dt_bias, ssd_a_log, ssd_d, ssd_norm, od_w_in, rw_mu, rw_w0, rw_w2, rw_a0, rw_a2, rw_g2, rw_k_k, rw_k_a, rw_r_k, rw_ln_w, rw_ln_b, rw_w_vin, rw_mu_v, rw_v0, rw_v2, hg_lb_raw, hg_norm):
    nb, seq, _ = x.shape
    depth = norm_mix_pre.shape[0]
    lp = PAD + N_META + seq
    rows = lp * nb
    tm = _row_tile(rows, 512)
    h = jnp.concatenate([jnp.zeros((PAD, nb, D_MODEL), x.dtype),
                         jnp.broadcast_to(meta.astype(x.dtype)[:, None], (N_META, nb, D_MODEL)),
                         x.transpose(1, 0, 2)], axis=0)
    h = h.reshape(rows, D_MODEL)

    lb_w = jax.nn.softmax(hg_lb_raw.astype(F32), axis=0)
    lb_table = jnp.cumsum(lb_w, axis=0) - lb_w[0]
    v_first = None
    s1 = S5_WIDTH
    s2 = s1 + SSD_INNER
    s3 = s2 + SSD_XBC
    for layer in range(depth):
        if layer % 2 == 0:
            e = layer // 2
            u, z, xbc, dtr = _norm_proj(h, norm_mix_pre[layer], _even_weight(ev_w_in[e]),
                                        ((0, s1, False), (s1, SSD_INNER, True), (s2, SSD_XBC, True),
                                         (s3, LANES, True)), tm, nb)
            tables = _s5_tables(s5_lam_re[e], s5_lam_im[e], s5_log_dt[e], s5_b_re[e], s5_b_im[e],
                                s5_c_re[e], s5_c_im[e])
            y_a = _s5(u, tables, s5_d[e], s5_w_glu[e], s5_b_glu[e], nb, lp, CHUNK)
            y_b = _ssd(z, xbc, dtr, ssd_conv_w[e], ssd_conv_b[e], ssd_dt_bias[e], ssd_a_log[e], ssd_d[e],
                       ssd_norm[e], nb, lp)
        else:
            o = layer // 2
            mu = jnp.zeros((RWKV_INP,), F32).at[:RWKV_IN].set(rw_mu[o])
            w_rw = jnp.zeros((D_MODEL, RWKV_INP), F32).at[:, :RWKV_IN].set(od_w_in[o][:, :RWKV_IN])
            if o > 0:
                mu = mu.at[RWKV_IN:RWKV_IN + LORA_V].set(rw_mu_v[o - 1])
                w_rw = w_rw.at[:, RWKV_IN:RWKV_IN + LORA_V].set(rw_w_vin[o - 1])
            w_in = jnp.concatenate([w_rw, od_w_in[o][:, RWKV_IN:]], axis=1).astype(BF16)
            p_rw, p_hg = _norm_proj(h, norm_mix_pre[layer], w_in,
                                    ((0, RWKV_INP, False), (RWKV_INP, HGRN_IN, True)), tm, nb)
            vres = None if o == 0 else (v_first, rw_v0[o - 1], rw_v2[o - 1])
            outs = _rwkv_prep(p_rw, mu, rw_w0[o], rw_w2[o], rw_a0[o], rw_a2[o], rw_g2[o], rw_k_k[o],
                              rw_k_a[o], rw_r_k[o], vres, tm, nb)
            kkb, wk, rv, g, bv = outs[:5]
            if o == 0:
                v_first = outs[5]
            o_rw = _rwkv_scan(kkb, wk, rv, nb, lp, 32)
            y_a = _rwkv_post(o_rw, bv, g, rw_ln_w[o], rw_ln_b[o], tm, nb)
            y_b = _hgrn(p_hg, lb_table[o], hg_norm[o], nb, lp)
        ka = y_a.shape[1]
        w_out = mix_w_out[layer].astype(BF16)
        h = _out_proj(y_a, y_b, w_out[:ka], w_out[ka:], norm_mix_post[layer], h, tm, nb)
        h = _ffn(h, norm_ffn_pre[layer], ffn_w_up[layer].astype(BF16), ffn_conv_w[layer], ffn_conv_b[layer],
                 ffn_w_down[layer].astype(BF16), norm_ffn_post[layer], tm, nb)
    return h.reshape(lp, nb, D_MODEL)[PAD + N_META:].transpose(1, 0, 2)
```

```python
import functools

import numpy as np
import jax
import jax.numpy as jnp
from jax import lax
from jax.experimental import pallas as pl
from jax.experimental.pallas import tpu as pltpu

F32 = jnp.float32
BF16 = jnp.bfloat16

D_MODEL = 1024
N_META = 16
CHUNK = 64
PAD = CHUNK - N_META
RMS_EPS = 1e-6

S5_WIDTH = 256
S5_GROUP = 16
S5_GROUPS = S5_WIDTH // S5_GROUP
S5_STATE = 64
S5_NS = S5_GROUPS * S5_STATE

SSD_HEADDIM = 64
SSD_INNER = 768
SSD_HEADS = SSD_INNER // SSD_HEADDIM
SSD_GROUPS = 2
SSD_HPG = SSD_HEADS // SSD_GROUPS
SSD_STATE = 128
SSD_CONV = 4
SSD_XBC = SSD_INNER + 2 * SSD_GROUPS * SSD_STATE
MIX_WIDTH = S5_WIDTH + SSD_INNER

RWKV_WIDTH = 512
RWKV_HEADDIM = 64
RWKV_HEADS = RWKV_WIDTH // RWKV_HEADDIM
LORA_W = 64
LORA_A = 64
LORA_V = 32
LORA_G = 128
GN_EPS = 64e-5
RWKV_IN = 3 * RWKV_WIDTH + LORA_W + LORA_A + LORA_G
RWKV_INP = RWKV_IN + 128

HGRN_WIDTH = 512
HGRN_HEADS = 4
HGRN_HEADDIM = HGRN_WIDTH // HGRN_HEADS
HGRN_IN = 4 * HGRN_WIDTH

D_FF = 2816
FFN_CONV = 3

LANES = 128
SUBLANES = 8
VMEM_LIMIT = 56 * 1024 * 1024


def _cparams(*sem):
    return pltpu.CompilerParams(dimension_semantics=sem, vmem_limit_bytes=VMEM_LIMIT)


def _row_tile(rows, target):
    best = SUBLANES
    for t in range(SUBLANES, min(rows, target) + 1, SUBLANES):
        if rows % t == 0:
            best = t
    return best


def _const_spec(shape):
    nd = len(shape)
    return pl.BlockSpec(shape, lambda *_: (0,) * nd, pipeline_mode=pl.Buffered(1))


def _dot(a, b):
    return jnp.dot(a.astype(BF16), b.astype(BF16), preferred_element_type=F32)


def _dot_nt(a, b):
    return lax.dot_general(a.astype(BF16), b.astype(BF16), (((1,), (1,)), ((), ())),
                           preferred_element_type=F32)


def _dot_tn(a, b):
    return lax.dot_general(a.astype(BF16), b.astype(BF16), (((0,), (0,)), ((), ())),
                           preferred_element_type=F32)


def _split3(x):
    hi = x.astype(BF16)
    r1 = x - hi.astype(F32)
    mid = r1.astype(BF16)
    lo = (r1 - mid.astype(F32)).astype(BF16)
    return hi, mid, lo


def _dot_sel_lhs(sel, x):
    hi, mid, lo = _split3(x)
    d = lambda p: jnp.dot(sel, p, preferred_element_type=F32)
    return d(hi) + d(mid) + d(lo)


def _dot_sel_rhs(x, sel):
    hi, mid, lo = _split3(x)
    d = lambda p: jnp.dot(p, sel, preferred_element_type=F32)
    return d(hi) + d(mid) + d(lo)


def _sigmoid(x):
    return 1.0 / (1.0 + jnp.exp(-x))


def _silu(x):
    return x * _sigmoid(x)


def _softplus(x):
    return jnp.maximum(x, 0.0) + jnp.log(1.0 + jnp.exp(-jnp.abs(x)))


GELU_C = 0.7978845608028654
GELU_A = 0.044715


def _gelu_tanh(x):
    return 0.5 * x * (1.0 + jnp.tanh(GELU_C * (x + GELU_A * (x * x * x))))


def _rms(x, g):
    return x * lax.rsqrt(jnp.mean(x * x, axis=-1, keepdims=True) + RMS_EPS) * g


def _keep_rows(tm, nb):
    row = pl.program_id(0) * tm + lax.broadcasted_iota(jnp.int32, (tm, 1), 0)
    return row >= PAD * nb


def _norm_proj_kernel(x_ref, g_ref, w_ref, *refs, splits, chunk, nb):
    o_refs, tiles_ref = refs[:len(splits)], refs[len(splits)]
    tm = x_ref.shape[0]
    hn = _rms(x_ref[...], g_ref[...]).astype(BF16)
    for o_ref, (start, width, by_batch) in zip(o_refs, splits):
        for c0 in range(0, width, chunk):
            cw = min(chunk, width - c0)
            res = jnp.dot(hn, w_ref[:, start + c0:start + c0 + cw], preferred_element_type=F32)
            if not by_batch:
                o_ref[:, c0:c0 + cw] = res
            else:
                for j in range(cw // LANES):
                    tiles_ref[c0 // LANES + j] = res[:, j * LANES:(j + 1) * LANES]
        if by_batch:
            for j in range(width // LANES):
                for b in range(nb):
                    o_ref[:, b * width + j * LANES:b * width + (j + 1) * LANES] = (
                        tiles_ref[j, pl.ds(b, tm // nb, stride=nb), :])


def _norm_proj(h, g, w, splits, tm, nb):
    rows = h.shape[0]
    n = w.shape[1]
    spec = lambda wd, bb: pl.BlockSpec((tm // nb, nb * wd) if bb else (tm, wd), lambda i: (i, 0))
    shape = lambda wd, bb: jax.ShapeDtypeStruct((rows // nb, nb * wd) if bb else (rows, wd), F32)
    max_w = max([wd for _, wd, bb in splits if bb] + [LANES])
    return pl.pallas_call(
        functools.partial(_norm_proj_kernel, splits=splits, chunk=512, nb=nb),
        grid=(rows // tm,),
        in_specs=[pl.BlockSpec((tm, D_MODEL), lambda i: (i, 0)),
                  _const_spec((1, D_MODEL)),
                  _const_spec((D_MODEL, n))],
        out_specs=[spec(wd, bb) for _, wd, bb in splits],
        out_shape=[shape(wd, bb) for _, wd, bb in splits],
        scratch_shapes=[pltpu.VMEM((max_w // LANES, tm, LANES), F32)],
        compiler_params=_cparams("arbitrary"),
        name="norm_proj",
    )(h, g.reshape(1, D_MODEL), w)


def _out_proj_kernel(ya_ref, yb_ref, wa_ref, wb_ref, g_ref, h_ref, o_ref, tiles_ref, *, tm, nb, kb):
    for j in range(kb // LANES):
        for b in range(nb):
            tiles_ref[j, pl.ds(b, tm // nb, stride=nb), :] = yb_ref[:, b * kb + j * LANES:b * kb + (j + 1) * LANES]
    yb = jnp.concatenate([tiles_ref[j] for j in range(kb // LANES)], axis=1)
    o = _dot(ya_ref[...], wa_ref[...]) + _dot(yb, wb_ref[...])
    upd = _rms(o, g_ref[...])
    o_ref[...] = h_ref[...] + jnp.where(_keep_rows(tm, nb), upd, 0.0)


def _out_proj(ya, yb, wa, wb, g, h, tm, nb):
    rows = h.shape[0]
    ka, kb = ya.shape[1], yb.shape[1] // nb
    return pl.pallas_call(
        functools.partial(_out_proj_kernel, tm=tm, nb=nb, kb=kb),
        grid=(rows // tm,),
        in_specs=[pl.BlockSpec((tm, ka), lambda i: (i, 0)),
                  pl.BlockSpec((tm // nb, nb * kb), lambda i: (i, 0)),
                  _const_spec((ka, D_MODEL)),
                  _const_spec((kb, D_MODEL)),
                  _const_spec((1, D_MODEL)),
                  pl.BlockSpec((tm, D_MODEL), lambda i: (i, 0))],
        out_specs=pl.BlockSpec((tm, D_MODEL), lambda i: (i, 0)),
        out_shape=jax.ShapeDtypeStruct((rows, D_MODEL), F32),
        scratch_shapes=[pltpu.VMEM((kb // LANES, tm, LANES), F32)],
        compiler_params=_cparams("arbitrary"),
        name="out_proj",
    )(ya, yb, wa, wb, g.reshape(1, D_MODEL), h)


def _ffn_kernel(h_ref, g1_ref, wup_ref, cw_ref, cb_ref, wdn_ref, g2_ref, o_ref, carry_ref, act_ref,
                *, tm, tf, nb):
    halo = (FFN_CONV - 1) * nb

    @pl.when(pl.program_id(0) == 0)
    def _():
        carry_ref[...] = jnp.zeros_like(carry_ref)

    x = h_ref[...]
    hn = _rms(x, g1_ref[...]).astype(BF16)
    for c in range(D_FF // tf):
        halves = []
        for part in range(2):
            col = part * D_FF + c * tf
            u = jnp.dot(hn, wup_ref[:, col:col + tf], preferred_element_type=F32)
            hist = carry_ref[:, col:col + tf]
            carry_ref[:, col:col + tf] = u[tm - halo:tm, :]
            prev2 = jnp.concatenate([hist, u[:tm - halo, :]], axis=0)
            prev1 = jnp.concatenate([hist[nb:, :], u[:tm - nb, :]], axis=0)
            cw = cw_ref[:, col:col + tf]
            halves.append(cw[0:1] * prev2 + cw[1:2] * prev1 + cw[2:3] * u + cb_ref[:, col:col + tf])
        gate, val = halves
        th = jnp.tanh(gate * (GELU_C + (GELU_C * GELU_A) * (gate * gate)))
        act_ref[:, c * tf:(c + 1) * tf] = (gate * (0.5 + 0.5 * th) * val).astype(BF16)
    acc = jnp.dot(act_ref[...], wdn_ref[...], preferred_element_type=F32)
    upd = _rms(acc, g2_ref[...])
    o_ref[...] = x + jnp.where(_keep_rows(tm, nb), upd, 0.0)


def _ffn(h, g1, wup, cw, cb, wdn, g2, tm, nb):
    rows = h.shape[0]
    tf = 256
    halo = (FFN_CONV - 1) * nb
    cw8 = jnp.zeros((SUBLANES, 2 * D_FF), F32).at[:FFN_CONV].set(cw)
    return pl.pallas_call(
        functools.partial(_ffn_kernel, tm=tm, tf=tf, nb=nb),
        grid=(rows // tm,),
        in_specs=[pl.BlockSpec((tm, D_MODEL), lambda i: (i, 0)),
                  _const_spec((1, D_MODEL)),
                  _const_spec((D_MODEL, 2 * D_FF)),
                  _const_spec((SUBLANES, 2 * D_FF)),
                  _const_spec((1, 2 * D_FF)),
                  _const_spec((D_FF, D_MODEL)),
                  _const_spec((1, D_MODEL))],
        out_specs=pl.BlockSpec((tm, D_MODEL), lambda i: (i, 0)),
        out_shape=jax.ShapeDtypeStruct((rows, D_MODEL), F32),
        scratch_shapes=[pltpu.VMEM((halo, 2 * D_FF), F32),
                        pltpu.VMEM((tm, D_FF), BF16)],
        compiler_params=_cparams("arbitrary"),
        name="conv_ffn",
    )(h, g1.reshape(1, D_MODEL), wup, cw8, cb.reshape(1, 2 * D_FF), wdn, g2.reshape(1, D_MODEL))


def _s5_kernel(u_ref, bmat_ref, are_ref, aim_ref, cmat_ref, d_ref, wg_ref, bg_ref, o_ref,
               h_ref, bu_ref, hs_ref, *, nb, q):
    @pl.when(pl.program_id(0) == 0)
    def _():
        h_ref[...] = jnp.zeros_like(h_ref)

    u = u_ref[...]
    bu_ref[...] = _dot(u, bmat_ref[...])
    a_re = jnp.broadcast_to(are_ref[...], (nb, S5_NS))
    a_im = jnp.broadcast_to(aim_ref[...], (nb, S5_NS))

    def step(t, carry):
        h_re, h_im = carry
        rows = pl.ds(pl.multiple_of(t * nb, nb), nb)
        n_re = a_re * h_re - a_im * h_im + bu_ref[rows, :S5_NS]
        n_im = a_re * h_im + a_im * h_re + bu_ref[rows, S5_NS:]
        hs_ref[rows, :S5_NS] = n_re
        hs_ref[rows, S5_NS:] = n_im
        return n_re, n_im

    h_re, h_im = lax.fori_loop(0, q, step, (h_ref[:, :S5_NS], h_ref[:, S5_NS:]), unroll=4)
    h_ref[:, :S5_NS] = h_re
    h_ref[:, S5_NS:] = h_im
    y = _gelu_tanh(_dot(hs_ref[...], cmat_ref[...]) + d_ref[...] * u)
    o_ref[...] = y * _sigmoid(_dot(y, wg_ref[...]) + bg_ref[...])


def _s5_tables(lam_re, lam_im, log_dt, b_re, b_im, c_re, c_im):
    lr, li = lam_re.astype(F32), lam_im.astype(F32)
    dt = jnp.exp(log_dt.astype(F32))[:, None]
    mag = jnp.exp(lr * dt)
    ab_re, ab_im = mag * jnp.cos(li * dt), mag * jnp.sin(li * dt)
    den = lr * lr + li * li
    zr, zi = ab_re - 1.0, ab_im
    f_re = (zr * lr + zi * li) / den
    f_im = (zi * lr - zr * li) / den
    br, bi = b_re.astype(F32), b_im.astype(F32)
    bb_re = f_re[..., None] * br - f_im[..., None] * bi
    bb_im = f_re[..., None] * bi + f_im[..., None] * br
    eye = jnp.eye(S5_GROUPS, dtype=F32)
    to_b = lambda t: jnp.einsum("gpc,gh->gchp", t, eye).reshape(S5_WIDTH, S5_NS)
    bmat = jnp.concatenate([to_b(bb_re), to_b(bb_im)], axis=1)
    to_c = lambda t: jnp.einsum("gcp,gh->gphc", t.astype(F32), eye).reshape(S5_NS, S5_WIDTH)
    cmat = jnp.concatenate([to_c(c_re), -to_c(c_im)], axis=0)
    return bmat.astype(BF16), ab_re.reshape(1, S5_NS), ab_im.reshape(1, S5_NS), cmat.astype(BF16)


def _s5(u, tables, d_skip, w_glu, b_glu, nb, lp, q):
    bmat, a_re, a_im, cmat = tables
    blk = pl.BlockSpec((q * nb, S5_WIDTH), lambda c: (c, 0))
    return pl.pallas_call(
        functools.partial(_s5_kernel, nb=nb, q=q),
        grid=(lp // q,),
        in_specs=[blk,
                  _const_spec((S5_WIDTH, 2 * S5_NS)),
                  _const_spec((1, S5_NS)),
                  _const_spec((1, S5_NS)),
                  _const_spec((2 * S5_NS, S5_WIDTH)),
                  _const_spec((1, S5_WIDTH)),
                  _const_spec((S5_WIDTH, S5_WIDTH)),
                  _const_spec((1, S5_WIDTH))],
        out_specs=blk,
        out_shape=jax.ShapeDtypeStruct((lp * nb, S5_WIDTH), F32),
        scratch_shapes=[pltpu.VMEM((nb, 2 * S5_NS), F32),
                        pltpu.VMEM((nb * q, 2 * S5_NS), F32),
                        pltpu.VMEM((nb * q, 2 * S5_NS), F32)],
        compiler_params=_cparams("arbitrary"),
        name="s5",
    )(u, bmat, a_re, a_im, cmat, d_skip.reshape(1, S5_WIDTH).astype(F32), w_glu.astype(BF16),
      b_glu.reshape(1, S5_WIDTH).astype(F32))


def _ssd_kernel(z_ref, xbc_ref, dt_ref, cw_ref, cb_ref, dtb_ref, alog_ref, dsk_ref, nw_ref, o_ref,
                carry_ref, ext_ref, st_ref, y_ref):
    c = pl.program_id(1)

    @pl.when(c == 0)
    def _():
        carry_ref[...] = jnp.zeros_like(carry_ref)
        st_ref[...] = jnp.zeros_like(st_ref)

    x = xbc_ref[...]
    ext_ref[0:SUBLANES, :] = carry_ref[...]
    ext_ref[SUBLANES:SUBLANES + CHUNK, :] = x
    carry_ref[...] = x[CHUNK - SUBLANES:CHUNK, :]
    cw = cw_ref[...]
    conv = cb_ref[...] + cw[SSD_CONV - 1:SSD_CONV] * x
    for k in range(SSD_CONV - 1):
        off = SUBLANES - (SSD_CONV - 1) + k
        conv = conv + cw[k:k + 1] * ext_ref[off:off + CHUNK, :]
    xa = _silu(conv)

    row = lax.broadcasted_iota(jnp.int32, (CHUNK, 1), 0)
    dt = _softplus(dt_ref[...] + dtb_ref[...])
    dt = jnp.where(jnp.logical_or(c > 0, row >= PAD), dt, 0.0)
    d = dt * (-jnp.exp(alog_ref[...]))
    li = lax.broadcasted_iota(jnp.int32, (CHUNK, CHUNK), 0)
    si = lax.broadcasted_iota(jnp.int32, (CHUNK, CHUNK), 1)
    causal = li >= si
    acum = _dot_sel_lhs(causal.astype(BF16), d)
    acum_t = _dot_sel_rhs(d.T, (li <= si).astype(BF16))

    for g in range(SSD_GROUPS):
        bg = xa[:, SSD_INNER + g * SSD_STATE:SSD_INNER + (g + 1) * SSD_STATE]
        cg = xa[:, SSD_INNER + (SSD_GROUPS + g) * SSD_STATE:SSD_INNER + (SSD_GROUPS + g + 1) * SSD_STATE]
        cb = _dot_nt(cg, bg)
        bg_t = bg.T
        for j in range(SSD_HPG):
            hd = g * SSD_HPG + j
            sl = slice(hd * SSD_HEADDIM, (hd + 1) * SSD_HEADDIM)
            xs = xa[:, sl]
            ac = acum[:, hd:hd + 1]
            seg = jnp.exp(jnp.minimum(ac - acum_t[hd:hd + 1, :], 0.0))
            m = jnp.where(causal, cb * seg, 0.0)
            xdt = xs * dt[:, hd:hd + 1]
            st = st_ref[hd]
            y = _dot(m, xdt) + _dot(cg, st) * jnp.exp(ac) + xs * dsk_ref[:, sl]
            y_ref[:, sl] = y
            a_last = acum[CHUNK - 1:CHUNK, hd:hd + 1]
            st_ref[hd] = st * jnp.exp(a_last) + _dot(bg_t, xdt * jnp.exp(a_last - ac))

    y = y_ref[...] * _silu(z_ref[...])
    gw = SSD_INNER // SSD_GROUPS
    for g in range(SSD_GROUPS):
        o_ref[:, g * gw:(g + 1) * gw] = _rms(y[:, g * gw:(g + 1) * gw], nw_ref[:, g * gw:(g + 1) * gw])


def _batch_chunk_spec(width):
    return pl.BlockSpec((CHUNK, width), lambda b, c: (c, b))


def _ssd(z, xbc, dtr, conv_w, conv_b, dt_bias, a_log, d_skip, norm_w, nb, lp):
    pad_h = lambda t: jnp.zeros((1, LANES), F32).at[0, :SSD_HEADS].set(t.astype(F32))
    cw8 = jnp.zeros((SUBLANES, SSD_XBC), F32).at[:SSD_CONV].set(conv_w.astype(F32))
    dsk = jnp.repeat(d_skip.astype(F32), SSD_HEADDIM).reshape(1, SSD_INNER)
    out = pl.pallas_call(
        _ssd_kernel,
        grid=(nb, lp // CHUNK),
        in_specs=[_batch_chunk_spec(SSD_INNER), _batch_chunk_spec(SSD_XBC), _batch_chunk_spec(LANES),
                  _const_spec((SUBLANES, SSD_XBC)), _const_spec((1, SSD_XBC)),
                  _const_spec((1, LANES)), _const_spec((1, LANES)),
                  _const_spec((1, SSD_INNER)), _const_spec((1, SSD_INNER))],
        out_specs=_batch_chunk_spec(SSD_INNER),
        out_shape=jax.ShapeDtypeStruct((lp, nb * SSD_INNER), F32),
        scratch_shapes=[pltpu.VMEM((SUBLANES, SSD_XBC), F32),
                        pltpu.VMEM((CHUNK + SUBLANES, SSD_XBC), F32),
                        pltpu.VMEM((SSD_HEADS, SSD_STATE, SSD_HEADDIM), F32),
                        pltpu.VMEM((CHUNK, SSD_INNER), F32)],
        compiler_params=_cparams("arbitrary", "arbitrary"),
        name="ssd",
    )(z, xbc, dtr, cw8, conv_b.reshape(1, SSD_XBC).astype(F32), pad_h(dt_bias), pad_h(a_log), dsk,
      norm_w.reshape(1, SSD_INNER).astype(F32))
    return out


HG_LEVELS = 6


def _hgrn_tables():
    e = np.zeros((2 + HG_LEVELS, CHUNK, CHUNK), np.float32)
    m = np.zeros((1 + HG_LEVELS, CHUNK, CHUNK), np.float32)
    idx = np.arange(CHUNK)
    e[0] = idx[None, :] <= idx[:, None]
    e[1] = idx[None, :] > idx[:, None]
    m[0] = np.eye(CHUNK)
    for k in range(1, HG_LEVELS + 1):
        half = 1 << (k - 1)
        for r in range(CHUNK):
            bound = ((r >> k) << k) + half - 1
            if r > bound:
                e[1 + k, r, bound + 1:r + 1] = 1.0
            else:
                e[1 + k, r, r + 1:bound + 1] = 1.0
        same = (idx[:, None] >> k) == (idx[None, :] >> k)
        upper = ((idx[:, None] >> (k - 1)) & 1) == 1
        lower = ((idx[None, :] >> (k - 1)) & 1) == 0
        m[k] = same & upper & lower
    return e.reshape(-1, CHUNK), m


def _hgrn_kernel(p_ref, lb_ref, nw_ref, e_ref, m_ref, o_ref, st_ref):
    @pl.when(pl.program_id(1) == 0)
    def _():
        st_ref[...] = jnp.zeros_like(st_ref)

    hd = HGRN_HEADDIM
    heads = range(HGRN_HEADS)
    wd = HGRN_WIDTH
    q = _silu(p_ref[:, 0:wd])
    forget = lb_ref[...] + (1.0 - lb_ref[...]) * _sigmoid(p_ref[:, wd:2 * wd])
    kf = 1.0 - forget
    eg = jnp.exp(_dot_sel_lhs(e_ref[...], jnp.log(forget)))
    hs = lambda x, h: x[:, h * hd:(h + 1) * hd]
    att = [m_ref[0] * _dot_nt(hs(q, h), hs(kf, h)) for h in heads]
    for k in range(1, HG_LEVELS + 1):
        ek = eg[(1 + k) * CHUNK:(2 + k) * CHUNK]
        qe, ke = q * ek, kf * ek
        att = [att[h] + m_ref[k] * _dot_nt(hs(qe, h), hs(ke, h)) for h in heads]
    qg = q * eg[0:CHUNK]
    kg = kf * eg[CHUNK:2 * CHUNK]
    iv = p_ref[:, 2 * wd:3 * wd]
    st = [st_ref[h] for h in heads]
    out = [_dot(att[h], hs(iv, h)) + _dot_nt(hs(qg, h), st[h]) for h in heads]
    for h in heads:
        st_ref[h] = st[h] * eg[CHUNK - 1:CHUNK, h * hd:(h + 1) * hd] + _dot_tn(hs(iv, h), hs(kg, h))
    og = _silu(p_ref[:, 3 * wd:4 * wd])
    for h in heads:
        o_ref[:, h * hd:(h + 1) * hd] = _rms(out[h], nw_ref[:, h * hd:(h + 1) * hd]) * hs(og, h)


def _hgrn(p, lb, norm_w, nb, lp):
    e_np, m_np = _hgrn_tables()
    out = pl.pallas_call(
        _hgrn_kernel,
        grid=(nb, lp // CHUNK),
        in_specs=[_batch_chunk_spec(HGRN_IN), _const_spec((1, HGRN_WIDTH)), _const_spec((1, HGRN_WIDTH)),
                  _const_spec(e_np.shape), _const_spec(m_np.shape)],
        out_specs=_batch_chunk_spec(HGRN_WIDTH),
        out_shape=jax.ShapeDtypeStruct((lp, nb * HGRN_WIDTH), F32),
        scratch_shapes=[pltpu.VMEM((HGRN_HEADS, HGRN_HEADDIM, HGRN_HEADDIM), F32)],
        compiler_params=_cparams("arbitrary", "arbitrary"),
        name="hgrn2",
    )(p, lb.reshape(1, HGRN_WIDTH).astype(F32),
      norm_w.reshape(1, HGRN_WIDTH).astype(F32), jnp.asarray(e_np, BF16), jnp.asarray(m_np, F32))
    return out


def _head_ones():
    idx = np.arange(RWKV_WIDTH) // RWKV_HEADDIM
    return (idx[:, None] == idx[None, :]).astype(np.float32)


def _rwkv_prep_kernel(*refs, tm, nb, has_vres):
    (p_ref, mu_ref, w0_ref, w2_ref, a0_ref, a2_ref, g2_ref, kk_ref, ka_ref, rk_ref, ones_ref) = refs[:11]
    if has_vres:
        vf_ref, v0_ref, v2_ref = refs[11:14]
        kkb_o, wk_o, rv_o, g_o, bv_o, carry_ref, ext_ref = refs[14:]
    else:
        kkb_o, wk_o, rv_o, g_o, bv_o, v_o, carry_ref, ext_ref = refs[11:]

    @pl.when(pl.program_id(0) == 0)
    def _():
        carry_ref[...] = jnp.zeros_like(carry_ref)

    p = p_ref[...]
    ext_ref[0:nb, :] = carry_ref[...]
    ext_ref[nb:nb + tm, :] = p
    carry_ref[...] = p[tm - nb:tm, :]
    prev = ext_ref[0:tm, :]
    ps = p + (prev - p) * mu_ref[...]

    wd = RWKV_WIDTH
    r, k, v = ps[:, :wd], ps[:, wd:2 * wd], ps[:, 2 * wd:3 * wd]
    pwa = ps[:, 3 * wd:3 * wd + LORA_W + LORA_A]
    pg = ps[:, 3 * wd + LORA_W + LORA_A:RWKV_IN]
    w_log = -_softplus(-(w0_ref[...] + _dot(jnp.tanh(pwa), w2_ref[...]))) - 0.5
    w = jnp.exp(-jnp.exp(w_log))
    a = _sigmoid(a0_ref[...] + _dot(pwa, a2_ref[...]))
    if has_vres:
        pv = ps[:, RWKV_IN:RWKV_INP]
        v = v + (vf_ref[...] - v) * _sigmoid(v0_ref[...] + _dot(pv, v2_ref[...]))
    else:
        v_o[...] = v
    g_o[...] = _dot(_sigmoid(pg), g2_ref[...])
    kk = k * kk_ref[...]
    ss = _dot_sel_rhs(kk * kk, ones_ref[...])
    kk = kk * lax.rsqrt(jnp.maximum(ss, 1e-24))
    k2 = k * (1.0 + (a - 1.0) * ka_ref[...])
    bonus = _dot_sel_rhs(r * k2 * rk_ref[...], ones_ref[...])
    bv_o[...] = bonus * v
    hd = RWKV_HEADDIM
    for o_ref, first, second in ((kkb_o, kk, kk * a), (wk_o, w, k2), (rv_o, r, v)):
        for h in range(RWKV_HEADS):
            both = jnp.concatenate([first[:, h * hd:(h + 1) * hd], second[:, h * hd:(h + 1) * hd]], axis=1)
            o_ref[:, h] = both.reshape(tm // nb, nb, 2 * hd)


def _rwkv_prep(p, mu, w0, w2, a0, a2, g2, k_k, k_a, r_k, vres, tm, nb):
    rows = p.shape[0]
    wd = RWKV_WIDTH
    row = lambda t: t.reshape(1, -1).astype(F32)
    w2p = jnp.zeros((LORA_W + LORA_A, wd), F32).at[:LORA_W].set(w2).astype(BF16)
    a2p = jnp.zeros((LORA_W + LORA_A, wd), F32).at[LORA_W:].set(a2).astype(BF16)
    args = [p, row(mu), row(w0), w2p, row(a0), a2p, g2.astype(BF16), row(k_k), row(k_a), row(r_k),
            jnp.asarray(_head_ones(), BF16)]
    specs = [pl.BlockSpec((tm, RWKV_INP), lambda i: (i, 0)), _const_spec((1, RWKV_INP)),
             _const_spec((1, wd)), _const_spec((LORA_W + LORA_A, wd)), _const_spec((1, wd)),
             _const_spec((LORA_W + LORA_A, wd)), _const_spec((LORA_G, wd)), _const_spec((1, wd)),
             _const_spec((1, wd)), _const_spec((1, wd)), _const_spec((wd, wd))]
    if vres is not None:
        v_first, v0, v2 = vres
        v2p = jnp.zeros((RWKV_INP - RWKV_IN, wd), F32).at[:LORA_V].set(v2).astype(BF16)
        args += [v_first, row(v0), v2p]
        specs += [pl.BlockSpec((tm, wd), lambda i: (i, 0)), _const_spec((1, wd)),
                  _const_spec((RWKV_INP - RWKV_IN, wd))]
    pair_shape = (rows // nb, RWKV_HEADS, nb, 2 * RWKV_HEADDIM)
    pair_spec = pl.BlockSpec((tm // nb,) + pair_shape[1:], lambda i: (i, 0, 0, 0))
    row_spec = pl.BlockSpec((tm, wd), lambda i: (i, 0))
    row_shape = jax.ShapeDtypeStruct((rows, wd), F32)
    n_row = 2 if vres is not None else 3
    return pl.pallas_call(
        functools.partial(_rwkv_prep_kernel, tm=tm, nb=nb, has_vres=vres is not None),
        grid=(rows // tm,),
        in_specs=specs,
        out_specs=[pair_spec] * 3 + [row_spec] * n_row,
        out_shape=[jax.ShapeDtypeStruct(pair_shape, F32)] * 3 + [row_shape] * n_row,
        scratch_shapes=[pltpu.VMEM((nb, RWKV_INP), F32),
                        pltpu.VMEM((tm + nb, RWKV_INP), F32)],
        compiler_params=_cparams("arbitrary"),
        name="rwkv_prep",
    )(*args)


RW_VLO = RWKV_HEADDIM // 2
RW_ACC = 4


RW_IO_UNROLL = 4


def _rwkv_scan_kernel(kkb_ref, wk_ref, rv_ref, o_ref,
                      m_ref, rs_ref, ws_ref, ks_ref, kks_ref, bs_ref, vs_ref, os_ref, *, tb, ni):
    @pl.when(pl.program_id(0) == 0)
    def _():
        m_ref[...] = jnp.zeros_like(m_ref)

    hd = RWKV_HEADDIM
    nl = 2 * ni
    shape = (RW_VLO, nl)

    low_lanes = lax.broadcasted_iota(jnp.int32, shape, 1) < ni

    def load(t, carry):
        for src, first, second in ((kkb_ref, kks_ref, bs_ref), (wk_ref, ws_ref, ks_ref), (rv_ref, rs_ref, None)):
            x = src[t].reshape(ni, 2 * hd)
            tt = jnp.concatenate([x, x], axis=0).T
            first[t] = tt[:hd]
            if second is not None:
                second[t] = tt[hd:]
            else:
                vs_ref[t] = jnp.where(low_lanes, tt[hd:hd + RW_VLO], tt[hd + RW_VLO:])
        return carry

    lax.fori_loop(0, tb, load, 0, unroll=RW_IO_UNROLL)

    def tree(parts):
        while len(parts) > 1:
            parts = [parts[i] + parts[i + 1] for i in range(0, len(parts), 2)]
        return parts[0]

    def step(t, carry):
        vt = vs_ref[t]
        acc = [None] * RW_ACC
        for k in range(hd):
            term = m_ref[k] * jnp.broadcast_to(kks_ref[t, k:k + 1, :], shape)
            acc[k % RW_ACC] = term if acc[k % RW_ACC] is None else acc[k % RW_ACC] + term
        sa = tree(acc)
        acc = [None] * RW_ACC
        for k in range(hd):
            bc = lambda ref: jnp.broadcast_to(ref[t, k:k + 1, :], shape)
            mk = m_ref[k] * bc(ws_ref) - sa * bc(bs_ref) + vt * bc(ks_ref)
            m_ref[k] = mk
            term = mk * bc(rs_ref)
            acc[k % RW_ACC] = term if acc[k % RW_ACC] is None else acc[k % RW_ACC] + term
        os_ref[t] = tree(acc)
        return carry

    lax.fori_loop(0, tb, step, 0, unroll=2)

    def store(t, carry):
        o = os_ref[t]
        o_ref[t] = jnp.concatenate([o[:, :ni], o[:, ni:]], axis=0).T.reshape(o_ref.shape[1:])
        return carry

    lax.fori_loop(0, tb, store, 0, unroll=RW_IO_UNROLL)


def _rwkv_scan(kkb, wk, rv, nb, lp, tb):
    nh, hd = RWKV_HEADS, RWKV_HEADDIM
    ni = nb * nh
    nl = 2 * ni
    in_spec = pl.BlockSpec((tb, nh, nb, 2 * hd), lambda i: (i, 0, 0, 0))
    return pl.pallas_call(
        functools.partial(_rwkv_scan_kernel, tb=tb, ni=ni),
        grid=(lp // tb,),
        in_specs=[in_spec] * 3,
        out_specs=pl.BlockSpec((tb, nh, nb, hd), lambda i: (i, 0, 0, 0)),
        out_shape=jax.ShapeDtypeStruct((lp, nh, nb, hd), F32),
        scratch_shapes=[pltpu.VMEM((hd, RW_VLO, nl), F32)] + [pltpu.VMEM((tb, hd, nl), F32)] * 5
                       + [pltpu.VMEM((tb, RW_VLO, nl), F32)] * 2,
        compiler_params=_cparams("arbitrary"),
        name="rwkv_scan",
    )(kkb, wk, rv)


def _rwkv_post_kernel(o_ref, bv_ref, g_ref, lw_ref, lb_ref, ones_ref, y_ref, *, tm):
    o = jnp.concatenate([o_ref[:, h].reshape(tm, RWKV_HEADDIM) for h in range(RWKV_HEADS)], axis=1)
    inv = 1.0 / RWKV_HEADDIM
    mean = _dot_sel_rhs(o, ones_ref[...]) * inv
    xc = o - mean
    var = _dot_sel_rhs(xc * xc, ones_ref[...]) * inv
    y = xc * lax.rsqrt(var + GN_EPS) * lw_ref[...] + lb_ref[...] + bv_ref[...]
    y_ref[...] = y * g_ref[...]


def _rwkv_post(o, bv, g, ln_w, ln_b, tm, nb):
    rows = bv.shape[0]
    wd = RWKV_WIDTH
    blk = pl.BlockSpec((tm, wd), lambda i: (i, 0))
    return pl.pallas_call(
        functools.partial(_rwkv_post_kernel, tm=tm),
        grid=(rows // tm,),
        in_specs=[pl.BlockSpec((tm // nb, RWKV_HEADS, nb, RWKV_HEADDIM), lambda i: (i, 0, 0, 0)),
                  blk, blk, _const_spec((1, wd)), _const_spec((1, wd)), _const_spec((wd, wd))],
        out_specs=blk,
        out_shape=jax.ShapeDtypeStruct((rows, wd), F32),
        compiler_params=_cparams("arbitrary"),
        name="rwkv_post",
    )(o, bv, g, ln_w.reshape(1, wd).astype(F32), ln_b.reshape(1, wd).astype(F32),
      jnp.asarray(_head_ones(), BF16))


def _even_weight(w_in):
    padw = jnp.zeros((D_MODEL, LANES - SSD_HEADS), w_in.dtype)
    return jnp.concatenate([w_in, padw], axis=1).astype(BF16)


def kernel(x, meta, norm_mix_pre, norm_mix_post, norm_ffn_pre, norm_ffn_post, mix_w_out, ffn_w_up, ffn_conv_w, ffn_conv_b, ffn_w_down, ev_w_in, s5_lam_re, s5_lam_im, s5_log_dt, s5_b_re, s5_b_im, s5_c_re, s5_c_im, s5_d, s5_w_glu, s5_b_glu, ssd_conv_w, ssd_conv_b, ssd_dt_bias, ssd_a_log, ssd_d, ssd_norm, od_w_in, rw_mu, rw_w0, rw_w2, rw_a0, rw_a2, rw_g2, rw_k_k, rw_k_a, rw_r_k, rw_ln_w, rw_ln_b, rw_w_vin, rw_mu_v, rw_v0, rw_v2, hg_lb_raw, hg_norm):
    nb, seq, _ = x.shape
    depth = norm_mix_pre.shape[0]
    lp = PAD + N_META + seq
    rows = lp * nb
    tm = _row_tile(rows, 512)
    h = jnp.concatenate([jnp.zeros((PAD, nb, D_MODEL), x.dtype),
                         jnp.broadcast_to(meta.astype(x.dtype)[:, None], (N_META, nb, D_MODEL)),
                         x.transpose(1, 0, 2)], axis=0)
    h = h.reshape(rows, D_MODEL)

    lb_w = jax.nn.softmax(hg_lb_raw.astype(F32), axis=0)
    lb_table = jnp.cumsum(lb_w, axis=0) - lb_w[0]
    v_first = None
    s1 = S5_WIDTH
    s2 = s1 + SSD_INNER
    s3 = s2 + SSD_XBC
    for layer in range(depth):
        if layer % 2 == 0:
            e = layer // 2
            u, z, xbc, dtr = _norm_proj(h, norm_mix_pre[layer], _even_weight(ev_w_in[e]),
                                        ((0, s1, False), (s1, SSD_INNER, True), (s2, SSD_XBC, True),
                                         (s3, LANES, True)), tm, nb)
            tables = _s5_tables(s5_lam_re[e], s5_lam_im[e], s5_log_dt[e], s5_b_re[e], s5_b_im[e],
                                s5_c_re[e], s5_c_im[e])
            y_a = _s5(u, tables, s5_d[e], s5_w_glu[e], s5_b_glu[e], nb, lp, CHUNK)
            y_b = _ssd(z, xbc, dtr, ssd_conv_w[e], ssd_conv_b[e], ssd_dt_bias[e], ssd_a_log[e], ssd_d[e],
                       ssd_norm[e], nb, lp)
        else:
            o = layer // 2
            mu = jnp.zeros((RWKV_INP,), F32).at[:RWKV_IN].set(rw_mu[o])
            w_rw = jnp.zeros((D_MODEL, RWKV_INP), F32).at[:, :RWKV_IN].set(od_w_in[o][:, :RWKV_IN])
            if o > 0:
                mu = mu.at[RWKV_IN:RWKV_IN + LORA_V].set(rw_mu_v[o - 1])
                w_rw = w_rw.at[:, RWKV_IN:RWKV_IN + LORA_V].set(rw_w_vin[o - 1])
            w_in = jnp.concatenate([w_rw, od_w_in[o][:, RWKV_IN:]], axis=1).astype(BF16)
            p_rw, p_hg = _norm_proj(h, norm_mix_pre[layer], w_in,
                                    ((0, RWKV_INP, False), (RWKV_INP, HGRN_IN, True)), tm, nb)
            vres = None if o == 0 else (v_first, rw_v0[o - 1], rw_v2[o - 1])
            outs = _rwkv_prep(p_rw, mu, rw_w0[o], rw_w2[o], rw_a0[o], rw_a2[o], rw_g2[o], rw_k_k[o],
                              rw_k_a[o], rw_r_k[o], vres, tm, nb)
            kkb, wk, rv, g, bv = outs[:5]
            if o == 0:
                v_first = outs[5]
            o_rw = _rwkv_scan(kkb, wk, rv, nb, lp, 32)
            y_a = _rwkv_post(o_rw, bv, g, rw_ln_w[o], rw_ln_b[o], tm, nb)
            y_b = _hgrn(p_hg, lb_table[o], hg_norm[o], nb, lp)
        ka = y_a.shape[1]
        w_out = mix_w_out[layer].astype(BF16)
        h = _out_proj(y_a, y_b, w_out[:ka], w_out[ka:], norm_mix_post[layer], h, tm, nb)
        h = _ffn(h, norm_ffn_pre[layer], ffn_w_up[layer].astype(BF16), ffn_conv_w[layer], ffn_conv_b[layer],
                 ffn_w_down[layer].astype(BF16), norm_ffn_post[layer], tm, nb)
    return h.reshape(lp, nb, D_MODEL)[PAD + N_META:].transpose(1, 0, 2)
```

```python
import functools

import numpy as np
import jax
import jax.numpy as jnp
from jax import lax
from jax.experimental import pallas as pl
from jax.experimental.pallas import tpu as pltpu

F32 = jnp.float32
BF16 = jnp.bfloat16

D_MODEL = 1024
N_META = 16
CHUNK = 64
PAD = CHUNK - N_META
RMS_EPS = 1e-6

S5_WIDTH = 256
S5_GROUP = 16
S5_GROUPS = S5_WIDTH // S5_GROUP
S5_STATE = 64
S5_NS = S5_GROUPS * S5_STATE

SSD_HEADDIM = 64
SSD_INNER = 768
SSD_HEADS = SSD_INNER // SSD_HEADDIM
SSD_GROUPS = 2
SSD_HPG = SSD_HEADS // SSD_GROUPS
SSD_STATE = 128
SSD_CONV = 4
SSD_XBC = SSD_INNER + 2 * SSD_GROUPS * SSD_STATE
MIX_WIDTH = S5_WIDTH + SSD_INNER

RWKV_WIDTH = 512
RWKV_HEADDIM = 64
RWKV_HEADS = RWKV_WIDTH // RWKV_HEADDIM
LORA_W = 64
LORA_A = 64
LORA_V = 32
LORA_G = 128
GN_EPS = 64e-5
RWKV_IN = 3 * RWKV_WIDTH + LORA_W + LORA_A + LORA_G
RWKV_INP = RWKV_IN + 128

HGRN_WIDTH = 512
HGRN_HEADS = 4
HGRN_HEADDIM = HGRN_WIDTH // HGRN_HEADS
HGRN_IN = 4 * HGRN_WIDTH

D_FF = 2816
FFN_CONV = 3

LANES = 128
SUBLANES = 8
VMEM_LIMIT = 56 * 1024 * 1024


def _cparams(*sem):
    return pltpu.CompilerParams(dimension_semantics=sem, vmem_limit_bytes=VMEM_LIMIT)


def _row_tile(rows, target):
    best = SUBLANES
    for t in range(SUBLANES, min(rows, target) + 1, SUBLANES):
        if rows % t == 0:
            best = t
    return best


def _const_spec(shape):
    nd = len(shape)
    return pl.BlockSpec(shape, lambda *_: (0,) * nd, pipeline_mode=pl.Buffered(1))


def _dot(a, b):
    return jnp.dot(a.astype(BF16), b.astype(BF16), preferred_element_type=F32)


def _dot_nt(a, b):
    return lax.dot_general(a.astype(BF16), b.astype(BF16), (((1,), (1,)), ((), ())),
                           preferred_element_type=F32)


def _dot_tn(a, b):
    return lax.dot_general(a.astype(BF16), b.astype(BF16), (((0,), (0,)), ((), ())),
                           preferred_element_type=F32)


def _split3(x):
    hi = x.astype(BF16)
    r1 = x - hi.astype(F32)
    mid = r1.astype(BF16)
    lo = (r1 - mid.astype(F32)).astype(BF16)
    return hi, mid, lo


def _dot_sel_lhs(sel, x):
    hi, mid, lo = _split3(x)
    d = lambda p: jnp.dot(sel, p, preferred_element_type=F32)
    return d(hi) + d(mid) + d(lo)


def _dot_sel_rhs(x, sel):
    hi, mid, lo = _split3(x)
    d = lambda p: jnp.dot(p, sel, preferred_element_type=F32)
    return d(hi) + d(mid) + d(lo)


def _sigmoid(x):
    return 1.0 / (1.0 + jnp.exp(-x))


def _silu(x):
    return x * _sigmoid(x)


def _softplus(x):
    return jnp.maximum(x, 0.0) + jnp.log(1.0 + jnp.exp(-jnp.abs(x)))


GELU_C = 0.7978845608028654
GELU_A = 0.044715


def _gelu_tanh(x):
    return 0.5 * x * (1.0 + jnp.tanh(GELU_C * (x + GELU_A * (x * x * x))))


def _rms(x, g):
    return x * lax.rsqrt(jnp.mean(x * x, axis=-1, keepdims=True) + RMS_EPS) * g


def _keep_rows(tm, nb):
    row = pl.program_id(0) * tm + lax.broadcasted_iota(jnp.int32, (tm, 1), 0)
    return row >= PAD * nb


def _norm_proj_kernel(x_ref, g_ref, w_ref, *refs, splits, chunk, nb):
    o_refs, tiles_ref = refs[:len(splits)], refs[len(splits)]
    tm = x_ref.shape[0]
    hn = _rms(x_ref[...], g_ref[...]).astype(BF16)
    for o_ref, (start, width, by_batch) in zip(o_refs, splits):
        for c0 in range(0, width, chunk):
            cw = min(chunk, width - c0)
            res = jnp.dot(hn, w_ref[:, start + c0:start + c0 + cw], preferred_element_type=F32)
            if not by_batch:
                o_ref[:, c0:c0 + cw] = res
            else:
                for j in range(cw // LANES):
                    tiles_ref[c0 // LANES + j] = res[:, j * LANES:(j + 1) * LANES]
        if by_batch:
            for j in range(width // LANES):
                for b in range(nb):
                    o_ref[:, b * width + j * LANES:b * width + (j + 1) * LANES] = (
                        tiles_ref[j, pl.ds(b, tm // nb, stride=nb), :])


def _norm_proj(h, g, w, splits, tm, nb):
    rows = h.shape[0]
    n = w.shape[1]
    spec = lambda wd, bb: pl.BlockSpec((tm // nb, nb * wd) if bb else (tm, wd), lambda i: (i, 0))
    shape = lambda wd, bb: jax.ShapeDtypeStruct((rows // nb, nb * wd) if bb else (rows, wd), F32)
    max_w = max([wd for _, wd, bb in splits if bb] + [LANES])
    return pl.pallas_call(
        functools.partial(_norm_proj_kernel, splits=splits, chunk=512, nb=nb),
        grid=(rows // tm,),
        in_specs=[pl.BlockSpec((tm, D_MODEL), lambda i: (i, 0)),
                  _const_spec((1, D_MODEL)),
                  _const_spec((D_MODEL, n))],
        out_specs=[spec(wd, bb) for _, wd, bb in splits],
        out_shape=[shape(wd, bb) for _, wd, bb in splits],
        scratch_shapes=[pltpu.VMEM((max_w // LANES, tm, LANES), F32)],
        compiler_params=_cparams("arbitrary"),
        name="norm_proj",
    )(h, g.reshape(1, D_MODEL), w)


def _out_proj_kernel(ya_ref, yb_ref, wa_ref, wb_ref, g_ref, h_ref, o_ref, tiles_ref, *, tm, nb, kb):
    for j in range(kb // LANES):
        for b in range(nb):
            tiles_ref[j, pl.ds(b, tm // nb, stride=nb), :] = yb_ref[:, b * kb + j * LANES:b * kb + (j + 1) * LANES]
    yb = jnp.concatenate([tiles_ref[j] for j in range(kb // LANES)], axis=1)
    o = _dot(ya_ref[...], wa_ref[...]) + _dot(yb, wb_ref[...])
    upd = _rms(o, g_ref[...])
    o_ref[...] = h_ref[...] + jnp.where(_keep_rows(tm, nb), upd, 0.0)


def _out_proj(ya, yb, wa, wb, g, h, tm, nb):
    rows = h.shape[0]
    ka, kb = ya.shape[1], yb.shape[1] // nb
    return pl.pallas_call(
        functools.partial(_out_proj_kernel, tm=tm, nb=nb, kb=kb),
        grid=(rows // tm,),
        in_specs=[pl.BlockSpec((tm, ka), lambda i: (i, 0)),
                  pl.BlockSpec((tm // nb, nb * kb), lambda i: (i, 0)),
                  _const_spec((ka, D_MODEL)),
                  _const_spec((kb, D_MODEL)),
                  _const_spec((1, D_MODEL)),
                  pl.BlockSpec((tm, D_MODEL), lambda i: (i, 0))],
        out_specs=pl.BlockSpec((tm, D_MODEL), lambda i: (i, 0)),
        out_shape=jax.ShapeDtypeStruct((rows, D_MODEL), F32),
        scratch_shapes=[pltpu.VMEM((kb // LANES, tm, LANES), F32)],
        compiler_params=_cparams("arbitrary"),
        name="out_proj",
    )(ya, yb, wa, wb, g.reshape(1, D_MODEL), h)


def _ffn_kernel(h_ref, g1_ref, wup_ref, cw_ref, cb_ref, wdn_ref, g2_ref, o_ref, carry_ref, act_ref,
                *, tm, tf, nb):
    halo = (FFN_CONV - 1) * nb

    @pl.when(pl.program_id(0) == 0)
    def _():
        carry_ref[...] = jnp.zeros_like(carry_ref)

    x = h_ref[...]
    hn = _rms(x, g1_ref[...]).astype(BF16)
    for c in range(D_FF // tf):
        halves = []
        for part in range(2):
            col = part * D_FF + c * tf
            u = jnp.dot(hn, wup_ref[:, col:col + tf], preferred_element_type=F32)
            hist = carry_ref[:, col:col + tf]
            carry_ref[:, col:col + tf] = u[tm - halo:tm, :]
            prev2 = jnp.concatenate([hist, u[:tm - halo, :]], axis=0)
            prev1 = jnp.concatenate([hist[nb:, :], u[:tm - nb, :]], axis=0)
            cw = cw_ref[:, col:col + tf]
            halves.append(cw[0:1] * prev2 + cw[1:2] * prev1 + cw[2:3] * u + cb_ref[:, col:col + tf])
        gate, val = halves
        th = jnp.tanh(gate * (GELU_C + (GELU_C * GELU_A) * (gate * gate)))
        act_ref[:, c * tf:(c + 1) * tf] = (gate * (0.5 + 0.5 * th) * val).astype(BF16)
    acc = jnp.dot(act_ref[...], wdn_ref[...], preferred_element_type=F32)
    upd = _rms(acc, g2_ref[...])
    o_ref[...] = x + jnp.where(_keep_rows(tm, nb), upd, 0.0)


def _ffn(h, g1, wup, cw, cb, wdn, g2, tm, nb):
    rows = h.shape[0]
    tf = 256
    halo = (FFN_CONV - 1) * nb
    cw8 = jnp.zeros((SUBLANES, 2 * D_FF), F32).at[:FFN_CONV].set(cw)
    return pl.pallas_call(
        functools.partial(_ffn_kernel, tm=tm, tf=tf, nb=nb),
        grid=(rows // tm,),
        in_specs=[pl.BlockSpec((tm, D_MODEL), lambda i: (i, 0)),
                  _const_spec((1, D_MODEL)),
                  _const_spec((D_MODEL, 2 * D_FF)),
                  _const_spec((SUBLANES, 2 * D_FF)),
                  _const_spec((1, 2 * D_FF)),
                  _const_spec((D_FF, D_MODEL)),
                  _const_spec((1, D_MODEL))],
        out_specs=pl.BlockSpec((tm, D_MODEL), lambda i: (i, 0)),
        out_shape=jax.ShapeDtypeStruct((rows, D_MODEL), F32),
        scratch_shapes=[pltpu.VMEM((halo, 2 * D_FF), F32),
                        pltpu.VMEM((tm, D_FF), BF16)],
        compiler_params=_cparams("arbitrary"),
        name="conv_ffn",
    )(h, g1.reshape(1, D_MODEL), wup, cw8, cb.reshape(1, 2 * D_FF), wdn, g2.reshape(1, D_MODEL))


def _s5_kernel(u_ref, bmat_ref, are_ref, aim_ref, cmat_ref, d_ref, wg_ref, bg_ref, o_ref,
               h_ref, bu_ref, hs_ref, *, nb, q):
    @pl.when(pl.program_id(0) == 0)
    def _():
        h_ref[...] = jnp.zeros_like(h_ref)

    u = u_ref[...]
    bu_ref[...] = _dot(u, bmat_ref[...])
    a_re = jnp.broadcast_to(are_ref[...], (nb, S5_NS))
    a_im = jnp.broadcast_to(aim_ref[...], (nb, S5_NS))

    def step(t, carry):
        h_re, h_im = carry
        rows = pl.ds(pl.multiple_of(t * nb, nb), nb)
        n_re = a_re * h_re - a_im * h_im + bu_ref[rows, :S5_NS]
        n_im = a_re * h_im + a_im * h_re + bu_ref[rows, S5_NS:]
        hs_ref[rows, :S5_NS] = n_re
        hs_ref[rows, S5_NS:] = n_im
        return n_re, n_im

    h_re, h_im = lax.fori_loop(0, q, step, (h_ref[:, :S5_NS], h_ref[:, S5_NS:]), unroll=4)
    h_ref[:, :S5_NS] = h_re
    h_ref[:, S5_NS:] = h_im
    y = _gelu_tanh(_dot(hs_ref[...], cmat_ref[...]) + d_ref[...] * u)
    o_ref[...] = y * _sigmoid(_dot(y, wg_ref[...]) + bg_ref[...])


def _s5_tables(lam_re, lam_im, log_dt, b_re, b_im, c_re, c_im):
    lr, li = lam_re.astype(F32), lam_im.astype(F32)
    dt = jnp.exp(log_dt.astype(F32))[:, None]
    mag = jnp.exp(lr * dt)
    ab_re, ab_im = mag * jnp.cos(li * dt), mag * jnp.sin(li * dt)
    den = lr * lr + li * li
    zr, zi = ab_re - 1.0, ab_im
    f_re = (zr * lr + zi * li) / den
    f_im = (zi * lr - zr * li) / den
    br, bi = b_re.astype(F32), b_im.astype(F32)
    bb_re = f_re[..., None] * br - f_im[..., None] * bi
    bb_im = f_re[..., None] * bi + f_im[..., None] * br
    eye = jnp.eye(S5_GROUPS, dtype=F32)
    to_b = lambda t: jnp.einsum("gpc,gh->gchp", t, eye).reshape(S5_WIDTH, S5_NS)
    bmat = jnp.concatenate([to_b(bb_re), to_b(bb_im)], axis=1)
    to_c = lambda t: jnp.einsum("gcp,gh->gphc", t.astype(F32), eye).reshape(S5_NS, S5_WIDTH)
    cmat = jnp.concatenate([to_c(c_re), -to_c(c_im)], axis=0)
    return bmat.astype(BF16), ab_re.reshape(1, S5_NS), ab_im.reshape(1, S5_NS), cmat.astype(BF16)


def _s5(u, tables, d_skip, w_glu, b_glu, nb, lp, q):
    bmat, a_re, a_im, cmat = tables
    blk = pl.BlockSpec((q * nb, S5_WIDTH), lambda c: (c, 0))
    return pl.pallas_call(
        functools.partial(_s5_kernel, nb=nb, q=q),
        grid=(lp // q,),
        in_specs=[blk,
                  _const_spec((S5_WIDTH, 2 * S5_NS)),
                  _const_spec((1, S5_NS)),
                  _const_spec((1, S5_NS)),
                  _const_spec((2 * S5_NS, S5_WIDTH)),
                  _const_spec((1, S5_WIDTH)),
                  _const_spec((S5_WIDTH, S5_WIDTH)),
                  _const_spec((1, S5_WIDTH))],
        out_specs=blk,
        out_shape=jax.ShapeDtypeStruct((lp * nb, S5_WIDTH), F32),
        scratch_shapes=[pltpu.VMEM((nb, 2 * S5_NS), F32),
                        pltpu.VMEM((nb * q, 2 * S5_NS), F32),
                        pltpu.VMEM((nb * q, 2 * S5_NS), F32)],
        compiler_params=_cparams("arbitrary"),
        name="s5",
    )(u, bmat, a_re, a_im, cmat, d_skip.reshape(1, S5_WIDTH).astype(F32), w_glu.astype(BF16),
      b_glu.reshape(1, S5_WIDTH).astype(F32))


SSD_PAIRS = SSD_HEADS // 2


def _ssd_tables():
    sel = np.zeros((LANES, SSD_INNER), np.float32)
    for h in range(SSD_HEADS):
        sel[h, h * SSD_HEADDIM:(h + 1) * SSD_HEADDIM] = 1.0
    return sel


def _ssd_kernel(z_ref, xbc_ref, dt_ref, cw_ref, cb_ref, dtb_ref, alog_ref, dsk_ref, nw_ref, sel_ref, o_ref,
                carry_ref, ext_ref, st_ref):
    c = pl.program_id(1)

    @pl.when(c == 0)
    def _():
        carry_ref[...] = jnp.zeros_like(carry_ref)
        st_ref[...] = jnp.zeros_like(st_ref)

    x = xbc_ref[...]
    ext_ref[0:SUBLANES, :] = carry_ref[...]
    ext_ref[SUBLANES:SUBLANES + CHUNK, :] = x
    carry_ref[...] = x[CHUNK - SUBLANES:CHUNK, :]
    cw = cw_ref[...]
    conv = cb_ref[...] + cw[SSD_CONV - 1:SSD_CONV] * x
    for k in range(SSD_CONV - 1):
        off = SUBLANES - (SSD_CONV - 1) + k
        conv = conv + cw[k:k + 1] * ext_ref[off:off + CHUNK, :]
    xa = _silu(conv)

    row = lax.broadcasted_iota(jnp.int32, (CHUNK, 1), 0)
    dt = _softplus(dt_ref[...] + dtb_ref[...])
    dt = jnp.where(jnp.logical_or(c > 0, row >= PAD), dt, 0.0)
    dt_e = _dot_sel_rhs(dt, sel_ref[...])
    d_e = dt_e * (-jnp.exp(alog_ref[...]))
    li = lax.broadcasted_iota(jnp.int32, (CHUNK, CHUNK), 0)
    si = lax.broadcasted_iota(jnp.int32, (CHUNK, CHUNK), 1)
    acum_e = _dot_sel_lhs((li >= si).astype(BF16), d_e)
    tri_t = (li <= si).astype(BF16)
    d_t = jnp.concatenate([d_e[:, h * SSD_HEADDIM:h * SSD_HEADDIM + 1] for h in range(SSD_HEADS)]
                          + [jnp.zeros((CHUNK, SUBLANES * 2 - SSD_HEADS), F32)], axis=1).T
    acum_t = _dot_sel_rhs(d_t, jnp.concatenate([tri_t, tri_t], axis=1))

    l2 = lax.broadcasted_iota(jnp.int32, (CHUNK, LANES), 0)
    lane = lax.broadcasted_iota(jnp.int32, (CHUNK, LANES), 1)
    low = lane < SSD_HEADDIM
    causal2 = l2 >= jnp.where(low, lane, lane - SSD_HEADDIM)

    groups, pairs = range(SSD_GROUPS), range(SSD_PAIRS)
    grp = lambda p: p // (SSD_PAIRS // SSD_GROUPS)
    ps = lambda p: slice(p * LANES, (p + 1) * LANES)
    bg = [xa[:, SSD_INNER + g * SSD_STATE:SSD_INNER + (g + 1) * SSD_STATE] for g in groups]
    cg = [xa[:, SSD_INNER + (SSD_GROUPS + g) * SSD_STATE:SSD_INNER + (SSD_GROUPS + g + 1) * SSD_STATE]
          for g in groups]
    cb2 = [_dot_nt(cg[g], jnp.concatenate([bg[g], bg[g]], axis=0)) for g in groups]
    bg_t = [bg[g].T for g in groups]
    xs = [xa[:, ps(p)] for p in pairs]
    ac = [acum_e[:, ps(p)] for p in pairs]
    a_last = [acum_e[CHUNK - 1:CHUNK, ps(p)] for p in pairs]
    ac_s = [jnp.where(low[0:1], acum_t[2 * p:2 * p + 1], acum_t[2 * p + 1:2 * p + 2]) for p in pairs]
    m = [jnp.where(causal2, cb2[grp(p)] * jnp.exp(jnp.minimum(ac[p] - ac_s[p], 0.0)), 0.0) for p in pairs]
    xdt = [xs[p] * dt_e[:, ps(p)] for p in pairs]
    xdt_bd = [jnp.concatenate([jnp.where(low, xdt[p], 0.0), jnp.where(low, 0.0, xdt[p])], axis=0)
              for p in pairs]
    st = [st_ref[p] for p in pairs]
    y_in = [_dot(m[p], xdt_bd[p]) for p in pairs]
    y_st = [_dot(cg[grp(p)], st[p]) for p in pairs]
    st_in = [_dot(bg_t[grp(p)], xdt[p] * jnp.exp(a_last[p] - ac[p])) for p in pairs]
    for p in pairs:
        st_ref[p] = st[p] * jnp.exp(a_last[p]) + st_in[p]
    y = jnp.concatenate([y_in[p] + y_st[p] * jnp.exp(ac[p]) for p in pairs], axis=1)
    y = (y + xa[:, :SSD_INNER] * dsk_ref[...]) * _silu(z_ref[...])
    gw = SSD_INNER // SSD_GROUPS
    for g in groups:
        o_ref[:, g * gw:(g + 1) * gw] = _rms(y[:, g * gw:(g + 1) * gw], nw_ref[:, g * gw:(g + 1) * gw])


def _batch_chunk_spec(width):
    return pl.BlockSpec((CHUNK, width), lambda b, c: (c, b))


def _ssd(z, xbc, dtr, conv_w, conv_b, dt_bias, a_log, d_skip, norm_w, nb, lp):
    pad_h = lambda t: jnp.zeros((1, LANES), F32).at[0, :SSD_HEADS].set(t.astype(F32))
    per_ch = lambda t: jnp.repeat(t.astype(F32), SSD_HEADDIM).reshape(1, SSD_INNER)
    cw8 = jnp.zeros((SUBLANES, SSD_XBC), F32).at[:SSD_CONV].set(conv_w.astype(F32))
    out = pl.pallas_call(
        _ssd_kernel,
        grid=(nb, lp // CHUNK),
        in_specs=[_batch_chunk_spec(SSD_INNER), _batch_chunk_spec(SSD_XBC), _batch_chunk_spec(LANES),
                  _const_spec((SUBLANES, SSD_XBC)), _const_spec((1, SSD_XBC)),
                  _const_spec((1, LANES)), _const_spec((1, SSD_INNER)),
                  _const_spec((1, SSD_INNER)), _const_spec((1, SSD_INNER)), _const_spec((LANES, SSD_INNER))],
        out_specs=_batch_chunk_spec(SSD_INNER),
        out_shape=jax.ShapeDtypeStruct((lp, nb * SSD_INNER), F32),
        scratch_shapes=[pltpu.VMEM((SUBLANES, SSD_XBC), F32),
                        pltpu.VMEM((CHUNK + SUBLANES, SSD_XBC), F32),
                        pltpu.VMEM((SSD_PAIRS, SSD_STATE, LANES), F32)],
        compiler_params=_cparams("arbitrary", "arbitrary"),
        name="ssd",
    )(z, xbc, dtr, cw8, conv_b.reshape(1, SSD_XBC).astype(F32), pad_h(dt_bias), per_ch(a_log),
      per_ch(d_skip), norm_w.reshape(1, SSD_INNER).astype(F32), jnp.asarray(_ssd_tables(), BF16))
    return out


HG_LEVELS = 6


def _hgrn_tables():
    e = np.zeros((2 + HG_LEVELS, CHUNK, CHUNK), np.float32)
    m = np.zeros((1 + HG_LEVELS, CHUNK, CHUNK), np.float32)
    idx = np.arange(CHUNK)
    e[0] = idx[None, :] <= idx[:, None]
    e[1] = idx[None, :] > idx[:, None]
    m[0] = np.eye(CHUNK)
    for k in range(1, HG_LEVELS + 1):
        half = 1 << (k - 1)
        for r in range(CHUNK):
            bound = ((r >> k) << k) + half - 1
            if r > bound:
                e[1 + k, r, bound + 1:r + 1] = 1.0
            else:
                e[1 + k, r, r + 1:bound + 1] = 1.0
        same = (idx[:, None] >> k) == (idx[None, :] >> k)
        upper = ((idx[:, None] >> (k - 1)) & 1) == 1
        lower = ((idx[None, :] >> (k - 1)) & 1) == 0
        m[k] = same & upper & lower
    return e.reshape(-1, CHUNK), m


def _hgrn_kernel(p_ref, lb_ref, nw_ref, e_ref, m_ref, o_ref, st_ref, *, bpb):
    @pl.when(pl.program_id(1) == 0)
    def _():
        st_ref[...] = jnp.zeros_like(st_ref)

    hd, wd = HGRN_HEADDIM, HGRN_WIDTH
    chains = range(bpb * HGRN_HEADS)
    part = lambda i: jnp.concatenate([p_ref[:, b * HGRN_IN + i * wd:b * HGRN_IN + (i + 1) * wd]
                                      for b in range(bpb)], axis=1)
    tile = lambda ref: jnp.concatenate([ref[...]] * bpb, axis=1)
    cs = lambda x, c: x[:, c * hd:(c + 1) * hd]
    q = _silu(part(0))
    lb = tile(lb_ref)
    forget = lb + (1.0 - lb) * _sigmoid(part(1))
    kf = 1.0 - forget
    eg = jnp.exp(_dot_sel_lhs(e_ref[...], jnp.log(forget)))
    att = [m_ref[0] * _dot_nt(cs(q, c), cs(kf, c)) for c in chains]
    for k in range(1, HG_LEVELS + 1):
        ek = eg[(1 + k) * CHUNK:(2 + k) * CHUNK]
        qe, ke = q * ek, kf * ek
        att = [att[c] + m_ref[k] * _dot_nt(cs(qe, c), cs(ke, c)) for c in chains]
    qg = q * eg[0:CHUNK]
    kg = kf * eg[CHUNK:2 * CHUNK]
    iv = part(2)
    st = [st_ref[c] for c in chains]
    out = [_dot(att[c], cs(iv, c)) + _dot_nt(cs(qg, c), st[c]) for c in chains]
    for c in chains:
        st_ref[c] = st[c] * cs(eg[CHUNK - 1:CHUNK], c) + _dot_tn(cs(iv, c), cs(kg, c))
    og = _silu(part(3))
    nw = tile(nw_ref)
    for c in chains:
        o_ref[:, c * hd:(c + 1) * hd] = _rms(out[c], cs(nw, c)) * cs(og, c)


HGRN_BPB = 4


def _hgrn(p, lb, norm_w, nb, lp):
    bpb = HGRN_BPB if nb % HGRN_BPB == 0 else 1
    e_np, m_np = _hgrn_tables()
    spec = lambda width: pl.BlockSpec((CHUNK, bpb * width), lambda b, c: (c, b))
    out = pl.pallas_call(
        functools.partial(_hgrn_kernel, bpb=bpb),
        grid=(nb // bpb, lp // CHUNK),
        in_specs=[spec(HGRN_IN), _const_spec((1, HGRN_WIDTH)), _const_spec((1, HGRN_WIDTH)),
                  _const_spec(e_np.shape), _const_spec(m_np.shape)],
        out_specs=spec(HGRN_WIDTH),
        out_shape=jax.ShapeDtypeStruct((lp, nb * HGRN_WIDTH), F32),
        scratch_shapes=[pltpu.VMEM((bpb * HGRN_HEADS, HGRN_HEADDIM, HGRN_HEADDIM), F32)],
        compiler_params=_cparams("arbitrary", "arbitrary"),
        name="hgrn2",
    )(p, lb.reshape(1, HGRN_WIDTH).astype(F32),
      norm_w.reshape(1, HGRN_WIDTH).astype(F32), jnp.asarray(e_np, BF16), jnp.asarray(m_np, F32))
    return out


def _head_ones():
    idx = np.arange(RWKV_WIDTH) // RWKV_HEADDIM
    return (idx[:, None] == idx[None, :]).astype(np.float32)


def _rwkv_prep_kernel(*refs, tm, nb, has_vres):
    (p_ref, mu_ref, w0_ref, w2_ref, a0_ref, a2_ref, g2_ref, kk_ref, ka_ref, rk_ref, ones_ref) = refs[:11]
    if has_vres:
        vf_ref, v0_ref, v2_ref = refs[11:14]
        kkb_o, wk_o, rv_o, g_o, bv_o, carry_ref, ext_ref = refs[14:]
    else:
        kkb_o, wk_o, rv_o, g_o, bv_o, v_o, carry_ref, ext_ref = refs[11:]

    @pl.when(pl.program_id(0) == 0)
    def _():
        carry_ref[...] = jnp.zeros_like(carry_ref)

    p = p_ref[...]
    ext_ref[0:nb, :] = carry_ref[...]
    ext_ref[nb:nb + tm, :] = p
    carry_ref[...] = p[tm - nb:tm, :]
    prev = ext_ref[0:tm, :]
    ps = p + (prev - p) * mu_ref[...]

    wd = RWKV_WIDTH
    r, k, v = ps[:, :wd], ps[:, wd:2 * wd], ps[:, 2 * wd:3 * wd]
    pwa = ps[:, 3 * wd:3 * wd + LORA_W + LORA_A]
    pg = ps[:, 3 * wd + LORA_W + LORA_A:RWKV_IN]
    w_log = -_softplus(-(w0_ref[...] + _dot(jnp.tanh(pwa), w2_ref[...]))) - 0.5
    w = jnp.exp(-jnp.exp(w_log))
    a = _sigmoid(a0_ref[...] + _dot(pwa, a2_ref[...]))
    if has_vres:
        pv = ps[:, RWKV_IN:RWKV_INP]
        v = v + (vf_ref[...] - v) * _sigmoid(v0_ref[...] + _dot(pv, v2_ref[...]))
    else:
        v_o[...] = v
    g_o[...] = _dot(_sigmoid(pg), g2_ref[...])
    kk = k * kk_ref[...]
    ss = _dot_sel_rhs(kk * kk, ones_ref[...])
    kk = kk * lax.rsqrt(jnp.maximum(ss, 1e-24))
    k2 = k * (1.0 + (a - 1.0) * ka_ref[...])
    bonus = _dot_sel_rhs(r * k2 * rk_ref[...], ones_ref[...])
    bv_o[...] = bonus * v
    hd = RWKV_HEADDIM
    for o_ref, first, second in ((kkb_o, kk, kk * a), (wk_o, w, k2), (rv_o, r, v)):
        for h in range(RWKV_HEADS):
            both = jnp.concatenate([first[:, h * hd:(h + 1) * hd], second[:, h * hd:(h + 1) * hd]], axis=1)
            o_ref[:, h] = both.reshape(tm // nb, nb, 2 * hd)


def _rwkv_prep(p, mu, w0, w2, a0, a2, g2, k_k, k_a, r_k, vres, tm, nb):
    rows = p.shape[0]
    wd = RWKV_WIDTH
    row = lambda t: t.reshape(1, -1).astype(F32)
    w2p = jnp.zeros((LORA_W + LORA_A, wd), F32).at[:LORA_W].set(w2).astype(BF16)
    a2p = jnp.zeros((LORA_W + LORA_A, wd), F32).at[LORA_W:].set(a2).astype(BF16)
    args = [p, row(mu), row(w0), w2p, row(a0), a2p, g2.astype(BF16), row(k_k), row(k_a), row(r_k),
            jnp.asarray(_head_ones(), BF16)]
    specs = [pl.BlockSpec((tm, RWKV_INP), lambda i: (i, 0)), _const_spec((1, RWKV_INP)),
             _const_spec((1, wd)), _const_spec((LORA_W + LORA_A, wd)), _const_spec((1, wd)),
             _const_spec((LORA_W + LORA_A, wd)), _const_spec((LORA_G, wd)), _const_spec((1, wd)),
             _const_spec((1, wd)), _const_spec((1, wd)), _const_spec((wd, wd))]
    if vres is not None:
        v_first, v0, v2 = vres
        v2p = jnp.zeros((RWKV_INP - RWKV_IN, wd), F32).at[:LORA_V].set(v2).astype(BF16)
        args += [v_first, row(v0), v2p]
        specs += [pl.BlockSpec((tm, wd), lambda i: (i, 0)), _const_spec((1, wd)),
                  _const_spec((RWKV_INP - RWKV_IN, wd))]
    pair_shape = (rows // nb, RWKV_HEADS, nb, 2 * RWKV_HEADDIM)
    pair_spec = pl.BlockSpec((tm // nb,) + pair_shape[1:], lambda i: (i, 0, 0, 0))
    row_spec = pl.BlockSpec((tm, wd), lambda i: (i, 0))
    row_shape = jax.ShapeDtypeStruct((rows, wd), F32)
    n_row = 2 if vres is not None else 3
    return pl.pallas_call(
        functools.partial(_rwkv_prep_kernel, tm=tm, nb=nb, has_vres=vres is not None),
        grid=(rows // tm,),
        in_specs=specs,
        out_specs=[pair_spec] * 3 + [row_spec] * n_row,
        out_shape=[jax.ShapeDtypeStruct(pair_shape, F32)] * 3 + [row_shape] * n_row,
        scratch_shapes=[pltpu.VMEM((nb, RWKV_INP), F32),
                        pltpu.VMEM((tm + nb, RWKV_INP), F32)],
        compiler_params=_cparams("arbitrary"),
        name="rwkv_prep",
    )(*args)


RW_VLO = RWKV_HEADDIM // 2
RW_ACC = 4


RW_IO_UNROLL = 4


def _rwkv_scan_kernel(kkb_ref, wk_ref, rv_ref, o_ref,
                      m_ref, rs_ref, ws_ref, ks_ref, kks_ref, bs_ref, vs_ref, os_ref, *, tb, ni):
    @pl.when(pl.program_id(0) == 0)
    def _():
        m_ref[...] = jnp.zeros_like(m_ref)

    hd = RWKV_HEADDIM
    nl = 2 * ni
    shape = (RW_VLO, nl)

    low_lanes = lax.broadcasted_iota(jnp.int32, shape, 1) < ni

    def load(t, carry):
        for src, first, second in ((kkb_ref, kks_ref, bs_ref), (wk_ref, ws_ref, ks_ref), (rv_ref, rs_ref, None)):
            x = src[t].reshape(ni, 2 * hd)
            tt = jnp.concatenate([x, x], axis=0).T
            first[t] = tt[:hd]
            if second is not None:
                second[t] = tt[hd:]
            else:
                vs_ref[t] = jnp.where(low_lanes, tt[hd:hd + RW_VLO], tt[hd + RW_VLO:])
        return carry

    lax.fori_loop(0, tb, load, 0, unroll=RW_IO_UNROLL)

    def tree(parts):
        while len(parts) > 1:
            parts = [parts[i] + parts[i + 1] for i in range(0, len(parts), 2)]
        return parts[0]

    def step(t, carry):
        vt = vs_ref[t]
        acc = [None] * RW_ACC
        for k in range(hd):
            term = m_ref[k] * jnp.broadcast_to(kks_ref[t, k:k + 1, :], shape)
            acc[k % RW_ACC] = term if acc[k % RW_ACC] is None else acc[k % RW_ACC] + term
        sa = tree(acc)
        acc = [None] * RW_ACC
        for k in range(hd):
            bc = lambda ref: jnp.broadcast_to(ref[t, k:k + 1, :], shape)
            mk = m_ref[k] * bc(ws_ref) - sa * bc(bs_ref) + vt * bc(ks_ref)
            m_ref[k] = mk
            term = mk * bc(rs_ref)
            acc[k % RW_ACC] = term if acc[k % RW_ACC] is None else acc[k % RW_ACC] + term
        os_ref[t] = tree(acc)
        return carry

    lax.fori_loop(0, tb, step, 0, unroll=2)

    def store(t, carry):
        o = os_ref[t]
        o_ref[t] = jnp.concatenate([o[:, :ni], o[:, ni:]], axis=0).T.reshape(o_ref.shape[1:])
        return carry

    lax.fori_loop(0, tb, store, 0, unroll=RW_IO_UNROLL)


def _rwkv_scan(kkb, wk, rv, nb, lp, tb):
    nh, hd = RWKV_HEADS, RWKV_HEADDIM
    ni = nb * nh
    nl = 2 * ni
    in_spec = pl.BlockSpec((tb, nh, nb, 2 * hd), lambda i: (i, 0, 0, 0))
    return pl.pallas_call(
        functools.partial(_rwkv_scan_kernel, tb=tb, ni=ni),
        grid=(lp // tb,),
        in_specs=[in_spec] * 3,
        out_specs=pl.BlockSpec((tb, nh, nb, hd), lambda i: (i, 0, 0, 0)),
        out_shape=jax.ShapeDtypeStruct((lp, nh, nb, hd), F32),
        scratch_shapes=[pltpu.VMEM((hd, RW_VLO, nl), F32)] + [pltpu.VMEM((tb, hd, nl), F32)] * 5
                       + [pltpu.VMEM((tb, RW_VLO, nl), F32)] * 2,
        compiler_params=_cparams("arbitrary"),
        name="rwkv_scan",
    )(kkb, wk, rv)


def _rwkv_post_kernel(o_ref, bv_ref, g_ref, lw_ref, lb_ref, ones_ref, y_ref, *, tm):
    o = jnp.concatenate([o_ref[:, h].reshape(tm, RWKV_HEADDIM) for h in range(RWKV_HEADS)], axis=1)
    inv = 1.0 / RWKV_HEADDIM
    mean = _dot_sel_rhs(o, ones_ref[...]) * inv
    xc = o - mean
    var = _dot_sel_rhs(xc * xc, ones_ref[...]) * inv
    y = xc * lax.rsqrt(var + GN_EPS) * lw_ref[...] + lb_ref[...] + bv_ref[...]
    y_ref[...] = y * g_ref[...]


def _rwkv_post(o, bv, g, ln_w, ln_b, tm, nb):
    rows = bv.shape[0]
    wd = RWKV_WIDTH
    blk = pl.BlockSpec((tm, wd), lambda i: (i, 0))
    return pl.pallas_call(
        functools.partial(_rwkv_post_kernel, tm=tm),
        grid=(rows // tm,),
        in_specs=[pl.BlockSpec((tm // nb, RWKV_HEADS, nb, RWKV_HEADDIM), lambda i: (i, 0, 0, 0)),
                  blk, blk, _const_spec((1, wd)), _const_spec((1, wd)), _const_spec((wd, wd))],
        out_specs=blk,
        out_shape=jax.ShapeDtypeStruct((rows, wd), F32),
        compiler_params=_cparams("arbitrary"),
        name="rwkv_post",
    )(o, bv, g, ln_w.reshape(1, wd).astype(F32), ln_b.reshape(1, wd).astype(F32),
      jnp.asarray(_head_ones(), BF16))


def _even_weight(w_in):
    padw = jnp.zeros((D_MODEL, LANES - SSD_HEADS), w_in.dtype)
    return jnp.concatenate([w_in, padw], axis=1).astype(BF16)


def kernel(x, meta, norm_mix_pre, norm_mix_post, norm_ffn_pre, norm_ffn_post, mix_w_out, ffn_w_up, ffn_conv_w, ffn_conv_b, ffn_w_down, ev_w_in, s5_lam_re, s5_lam_im, s5_log_dt, s5_b_re, s5_b_im, s5_c_re, s5_c_im, s5_d, s5_w_glu, s5_b_glu, ssd_conv_w, ssd_conv_b, ssd_dt_bias, ssd_a_log, ssd_d, ssd_norm, od_w_in, rw_mu, rw_w0, rw_w2, rw_a0, rw_a2, rw_g2, rw_k_k, rw_k_a, rw_r_k, rw_ln_w, rw_ln_b, rw_w_vin, rw_mu_v, rw_v0, rw_v2, hg_lb_raw, hg_norm):
    nb, seq, _ = x.shape
    depth = norm_mix_pre.shape[0]
    lp = PAD + N_META + seq
    rows = lp * nb
    tm = _row_tile(rows, 512)
    h = jnp.concatenate([jnp.zeros((PAD, nb, D_MODEL), x.dtype),
                         jnp.broadcast_to(meta.astype(x.dtype)[:, None], (N_META, nb, D_MODEL)),
                         x.transpose(1, 0, 2)], axis=0)
    h = h.reshape(rows, D_MODEL)

    lb_w = jax.nn.softmax(hg_lb_raw.astype(F32), axis=0)
    lb_table = jnp.cumsum(lb_w, axis=0) - lb_w[0]
    v_first = None
    s1 = S5_WIDTH
    s2 = s1 + SSD_INNER
    s3 = s2 + SSD_XBC
    for layer in range(depth):
        if layer % 2 == 0:
            e = layer // 2
            u, z, xbc, dtr = _norm_proj(h, norm_mix_pre[layer], _even_weight(ev_w_in[e]),
                                        ((0, s1, False), (s1, SSD_INNER, True), (s2, SSD_XBC, True),
                                         (s3, LANES, True)), tm, nb)
            tables = _s5_tables(s5_lam_re[e], s5_lam_im[e], s5_log_dt[e], s5_b_re[e], s5_b_im[e],
                                s5_c_re[e], s5_c_im[e])
            y_a = _s5(u, tables, s5_d[e], s5_w_glu[e], s5_b_glu[e], nb, lp, CHUNK)
            y_b = _ssd(z, xbc, dtr, ssd_conv_w[e], ssd_conv_b[e], ssd_dt_bias[e], ssd_a_log[e], ssd_d[e],
                       ssd_norm[e], nb, lp)
        else:
            o = layer // 2
            mu = jnp.zeros((RWKV_INP,), F32).at[:RWKV_IN].set(rw_mu[o])
            w_rw = jnp.zeros((D_MODEL, RWKV_INP), F32).at[:, :RWKV_IN].set(od_w_in[o][:, :RWKV_IN])
            if o > 0:
                mu = mu.at[RWKV_IN:RWKV_IN + LORA_V].set(rw_mu_v[o - 1])
                w_rw = w_rw.at[:, RWKV_IN:RWKV_IN + LORA_V].set(rw_w_vin[o - 1])
            w_in = jnp.concatenate([w_rw, od_w_in[o][:, RWKV_IN:]], axis=1).astype(BF16)
            p_rw, p_hg = _norm_proj(h, norm_mix_pre[layer], w_in,
                                    ((0, RWKV_INP, False), (RWKV_INP, HGRN_IN, True)), tm, nb)
            vres = None if o == 0 else (v_first, rw_v0[o - 1], rw_v2[o - 1])
            outs = _rwkv_prep(p_rw, mu, rw_w0[o], rw_w2[o], rw_a0[o], rw_a2[o], rw_g2[o], rw_k_k[o],
                              rw_k_a[o], rw_r_k[o], vres, tm, nb)
            kkb, wk, rv, g, bv = outs[:5]
            if o == 0:
                v_first = outs[5]
            o_rw = _rwkv_scan(kkb, wk, rv, nb, lp, 32)
            y_a = _rwkv_post(o_rw, bv, g, rw_ln_w[o], rw_ln_b[o], tm, nb)
            y_b = _hgrn(p_hg, lb_table[o], hg_norm[o], nb, lp)
        ka = y_a.shape[1]
        w_out = mix_w_out[layer].astype(BF16)
        h = _out_proj(y_a, y_b, w_out[:ka], w_out[ka:], norm_mix_post[layer], h, tm, nb)
        h = _ffn(h, norm_ffn_pre[layer], ffn_w_up[layer].astype(BF16), ffn_conv_w[layer], ffn_conv_b[layer],
                 ffn_w_down[layer].astype(BF16), norm_ffn_post[layer], tm, nb)
    return h.reshape(lp, nb, D_MODEL)[PAD + N_META:].transpose(1, 0, 2)
```

```python
import functools

import numpy as np
import jax
import jax.numpy as jnp
from jax import lax
from jax.experimental import pallas as pl
from jax.experimental.pallas import tpu as pltpu

F32 = jnp.float32
BF16 = jnp.bfloat16

D_MODEL = 1024
N_META = 16
CHUNK = 64
PAD = CHUNK - N_META
RMS_EPS = 1e-6

S5_WIDTH = 256
S5_GROUP = 16
S5_GROUPS = S5_WIDTH // S5_GROUP
S5_STATE = 64
S5_NS = S5_GROUPS * S5_STATE

SSD_HEADDIM = 64
SSD_INNER = 768
SSD_HEADS = SSD_INNER // SSD_HEADDIM
SSD_GROUPS = 2
SSD_HPG = SSD_HEADS // SSD_GROUPS
SSD_STATE = 128
SSD_CONV = 4
SSD_XBC = SSD_INNER + 2 * SSD_GROUPS * SSD_STATE
MIX_WIDTH = S5_WIDTH + SSD_INNER

RWKV_WIDTH = 512
RWKV_HEADDIM = 64
RWKV_HEADS = RWKV_WIDTH // RWKV_HEADDIM
LORA_W = 64
LORA_A = 64
LORA_V = 32
LORA_G = 128
GN_EPS = 64e-5
RWKV_IN = 3 * RWKV_WIDTH + LORA_W + LORA_A + LORA_G
RWKV_INP = RWKV_IN + 128

HGRN_WIDTH = 512
HGRN_HEADS = 4
HGRN_HEADDIM = HGRN_WIDTH // HGRN_HEADS
HGRN_IN = 4 * HGRN_WIDTH

D_FF = 2816
FFN_CONV = 3

LANES = 128
SUBLANES = 8
VMEM_LIMIT = 56 * 1024 * 1024


def _cparams(*sem):
    return pltpu.CompilerParams(dimension_semantics=sem, vmem_limit_bytes=VMEM_LIMIT)


def _row_tile(rows, target):
    best = SUBLANES
    for t in range(SUBLANES, min(rows, target) + 1, SUBLANES):
        if rows % t == 0:
            best = t
    return best


def _const_spec(shape):
    nd = len(shape)
    return pl.BlockSpec(shape, lambda *_: (0,) * nd, pipeline_mode=pl.Buffered(1))


def _dot(a, b):
    return jnp.dot(a.astype(BF16), b.astype(BF16), preferred_element_type=F32)


def _dot_nt(a, b):
    return lax.dot_general(a.astype(BF16), b.astype(BF16), (((1,), (1,)), ((), ())),
                           preferred_element_type=F32)


def _dot_tn(a, b):
    return lax.dot_general(a.astype(BF16), b.astype(BF16), (((0,), (0,)), ((), ())),
                           preferred_element_type=F32)


def _split3(x):
    hi = x.astype(BF16)
    r1 = x - hi.astype(F32)
    mid = r1.astype(BF16)
    lo = (r1 - mid.astype(F32)).astype(BF16)
    return hi, mid, lo


def _dot_sel_lhs(sel, x):
    hi, mid, lo = _split3(x)
    d = lambda p: jnp.dot(sel, p, preferred_element_type=F32)
    return d(hi) + d(mid) + d(lo)


def _dot_sel_rhs(x, sel):
    hi, mid, lo = _split3(x)
    d = lambda p: jnp.dot(p, sel, preferred_element_type=F32)
    return d(hi) + d(mid) + d(lo)


def _sigmoid(x):
    return 1.0 / (1.0 + jnp.exp(-x))


def _silu(x):
    return x * _sigmoid(x)


def _softplus(x):
    return jnp.maximum(x, 0.0) + jnp.log(1.0 + jnp.exp(-jnp.abs(x)))


GELU_C = 0.7978845608028654
GELU_A = 0.044715


def _gelu_tanh(x):
    return 0.5 * x * (1.0 + jnp.tanh(GELU_C * (x + GELU_A * (x * x * x))))


def _rms(x, g):
    return x * lax.rsqrt(jnp.mean(x * x, axis=-1, keepdims=True) + RMS_EPS) * g


def _keep_rows(tm, nb):
    row = pl.program_id(0) * tm + lax.broadcasted_iota(jnp.int32, (tm, 1), 0)
    return row >= PAD * nb


def _norm_proj_kernel(x_ref, g_ref, w_ref, *refs, splits, chunk, nb):
    o_refs, tiles_ref = refs[:len(splits)], refs[len(splits)]
    tm = x_ref.shape[0]
    hn = _rms(x_ref[...], g_ref[...]).astype(BF16)
    for o_ref, (start, width, by_batch) in zip(o_refs, splits):
        for c0 in range(0, width, chunk):
            cw = min(chunk, width - c0)
            res = jnp.dot(hn, w_ref[:, start + c0:start + c0 + cw], preferred_element_type=F32)
            if not by_batch:
                o_ref[:, c0:c0 + cw] = res
            else:
                for j in range(cw // LANES):
                    tiles_ref[c0 // LANES + j] = res[:, j * LANES:(j + 1) * LANES]
        if by_batch:
            for j in range(width // LANES):
                for b in range(nb):
                    o_ref[:, b * width + j * LANES:b * width + (j + 1) * LANES] = (
                        tiles_ref[j, pl.ds(b, tm // nb, stride=nb), :])


def _norm_proj(h, g, w, splits, tm, nb):
    rows = h.shape[0]
    n = w.shape[1]
    spec = lambda wd, bb: pl.BlockSpec((tm // nb, nb * wd) if bb else (tm, wd), lambda i: (i, 0))
    shape = lambda wd, bb: jax.ShapeDtypeStruct((rows // nb, nb * wd) if bb else (rows, wd), F32)
    max_w = max([wd for _, wd, bb in splits if bb] + [LANES])
    return pl.pallas_call(
        functools.partial(_norm_proj_kernel, splits=splits, chunk=512, nb=nb),
        grid=(rows // tm,),
        in_specs=[pl.BlockSpec((tm, D_MODEL), lambda i: (i, 0)),
                  _const_spec((1, D_MODEL)),
                  _const_spec((D_MODEL, n))],
        out_specs=[spec(wd, bb) for _, wd, bb in splits],
        out_shape=[shape(wd, bb) for _, wd, bb in splits],
        scratch_shapes=[pltpu.VMEM((max_w // LANES, tm, LANES), F32)],
        compiler_params=_cparams("arbitrary"),
        name="norm_proj",
    )(h, g.reshape(1, D_MODEL), w)


def _out_proj_kernel(ya_ref, yb_ref, wa_ref, wb_ref, g_ref, h_ref, o_ref, tiles_ref, *, tm, nb, kb):
    for j in range(kb // LANES):
        for b in range(nb):
            tiles_ref[j, pl.ds(b, tm // nb, stride=nb), :] = yb_ref[:, b * kb + j * LANES:b * kb + (j + 1) * LANES]
    yb = jnp.concatenate([tiles_ref[j] for j in range(kb // LANES)], axis=1)
    o = _dot(ya_ref[...], wa_ref[...]) + _dot(yb, wb_ref[...])
    upd = _rms(o, g_ref[...])
    o_ref[...] = h_ref[...] + jnp.where(_keep_rows(tm, nb), upd, 0.0)


def _out_proj(ya, yb, wa, wb, g, h, tm, nb):
    rows = h.shape[0]
    ka, kb = ya.shape[1], yb.shape[1] // nb
    return pl.pallas_call(
        functools.partial(_out_proj_kernel, tm=tm, nb=nb, kb=kb),
        grid=(rows // tm,),
        in_specs=[pl.BlockSpec((tm, ka), lambda i: (i, 0)),
                  pl.BlockSpec((tm // nb, nb * kb), lambda i: (i, 0)),
                  _const_spec((ka, D_MODEL)),
                  _const_spec((kb, D_MODEL)),
                  _const_spec((1, D_MODEL)),
                  pl.BlockSpec((tm, D_MODEL), lambda i: (i, 0))],
        out_specs=pl.BlockSpec((tm, D_MODEL), lambda i: (i, 0)),
        out_shape=jax.ShapeDtypeStruct((rows, D_MODEL), F32),
        scratch_shapes=[pltpu.VMEM((kb // LANES, tm, LANES), F32)],
        compiler_params=_cparams("arbitrary"),
        name="out_proj",
    )(ya, yb, wa, wb, g.reshape(1, D_MODEL), h)


def _ffn_kernel(h_ref, g1_ref, wup_ref, cw_ref, cb_ref, wdn_ref, g2_ref, o_ref, carry_ref, act_ref,
                *, tm, tf, nb):
    halo = (FFN_CONV - 1) * nb

    @pl.when(pl.program_id(0) == 0)
    def _():
        carry_ref[...] = jnp.zeros_like(carry_ref)

    x = h_ref[...]
    hn = _rms(x, g1_ref[...]).astype(BF16)
    for c in range(D_FF // tf):
        halves = []
        for part in range(2):
            col = part * D_FF + c * tf
            u = jnp.dot(hn, wup_ref[:, col:col + tf], preferred_element_type=F32)
            hist = carry_ref[:, col:col + tf]
            carry_ref[:, col:col + tf] = u[tm - halo:tm, :]
            prev2 = jnp.concatenate([hist, u[:tm - halo, :]], axis=0)
            prev1 = jnp.concatenate([hist[nb:, :], u[:tm - nb, :]], axis=0)
            cw = cw_ref[:, col:col + tf]
            halves.append(cw[0:1] * prev2 + cw[1:2] * prev1 + cw[2:3] * u + cb_ref[:, col:col + tf])
        gate, val = halves
        th = jnp.tanh(gate * (GELU_C + (GELU_C * GELU_A) * (gate * gate)))
        act_ref[:, c * tf:(c + 1) * tf] = (gate * (0.5 + 0.5 * th) * val).astype(BF16)
    acc = jnp.dot(act_ref[...], wdn_ref[...], preferred_element_type=F32)
    upd = _rms(acc, g2_ref[...])
    o_ref[...] = x + jnp.where(_keep_rows(tm, nb), upd, 0.0)


def _ffn(h, g1, wup, cw, cb, wdn, g2, tm, nb):
    rows = h.shape[0]
    tf = 256
    halo = (FFN_CONV - 1) * nb
    cw8 = jnp.zeros((SUBLANES, 2 * D_FF), F32).at[:FFN_CONV].set(cw)
    return pl.pallas_call(
        functools.partial(_ffn_kernel, tm=tm, tf=tf, nb=nb),
        grid=(rows // tm,),
        in_specs=[pl.BlockSpec((tm, D_MODEL), lambda i: (i, 0)),
                  _const_spec((1, D_MODEL)),
                  _const_spec((D_MODEL, 2 * D_FF)),
                  _const_spec((SUBLANES, 2 * D_FF)),
                  _const_spec((1, 2 * D_FF)),
                  _const_spec((D_FF, D_MODEL)),
                  _const_spec((1, D_MODEL))],
        out_specs=pl.BlockSpec((tm, D_MODEL), lambda i: (i, 0)),
        out_shape=jax.ShapeDtypeStruct((rows, D_MODEL), F32),
        scratch_shapes=[pltpu.VMEM((halo, 2 * D_FF), F32),
                        pltpu.VMEM((tm, D_FF), BF16)],
        compiler_params=_cparams("arbitrary"),
        name="conv_ffn",
    )(h, g1.reshape(1, D_MODEL), wup, cw8, cb.reshape(1, 2 * D_FF), wdn, g2.reshape(1, D_MODEL))


def _s5_kernel(u_ref, bmat_ref, are_ref, aim_ref, cmat_ref, d_ref, wg_ref, bg_ref, o_ref,
               h_ref, bu_ref, hs_ref, *, nb, q):
    @pl.when(pl.program_id(0) == 0)
    def _():
        h_ref[...] = jnp.zeros_like(h_ref)

    u = u_ref[...]
    bu_ref[...] = _dot(u, bmat_ref[...])
    a_re = jnp.broadcast_to(are_ref[...], (nb, S5_NS))
    a_im = jnp.broadcast_to(aim_ref[...], (nb, S5_NS))

    def step(t, carry):
        h_re, h_im = carry
        rows = pl.ds(pl.multiple_of(t * nb, nb), nb)
        n_re = a_re * h_re - a_im * h_im + bu_ref[rows, :S5_NS]
        n_im = a_re * h_im + a_im * h_re + bu_ref[rows, S5_NS:]
        hs_ref[rows, :S5_NS] = n_re
        hs_ref[rows, S5_NS:] = n_im
        return n_re, n_im

    h_re, h_im = lax.fori_loop(0, q, step, (h_ref[:, :S5_NS], h_ref[:, S5_NS:]), unroll=4)
    h_ref[:, :S5_NS] = h_re
    h_ref[:, S5_NS:] = h_im
    y = _gelu_tanh(_dot(hs_ref[...], cmat_ref[...]) + d_ref[...] * u)
    o_ref[...] = y * _sigmoid(_dot(y, wg_ref[...]) + bg_ref[...])


def _s5_tables(lam_re, lam_im, log_dt, b_re, b_im, c_re, c_im):
    lr, li = lam_re.astype(F32), lam_im.astype(F32)
    dt = jnp.exp(log_dt.astype(F32))[:, None]
    mag = jnp.exp(lr * dt)
    ab_re, ab_im = mag * jnp.cos(li * dt), mag * jnp.sin(li * dt)
    den = lr * lr + li * li
    zr, zi = ab_re - 1.0, ab_im
    f_re = (zr * lr + zi * li) / den
    f_im = (zi * lr - zr * li) / den
    br, bi = b_re.astype(F32), b_im.astype(F32)
    bb_re = f_re[..., None] * br - f_im[..., None] * bi
    bb_im = f_re[..., None] * bi + f_im[..., None] * br
    eye = jnp.eye(S5_GROUPS, dtype=F32)
    to_b = lambda t: jnp.einsum("gpc,gh->gchp", t, eye).reshape(S5_WIDTH, S5_NS)
    bmat = jnp.concatenate([to_b(bb_re), to_b(bb_im)], axis=1)
    to_c = lambda t: jnp.einsum("gcp,gh->gphc", t.astype(F32), eye).reshape(S5_NS, S5_WIDTH)
    cmat = jnp.concatenate([to_c(c_re), -to_c(c_im)], axis=0)
    return bmat.astype(BF16), ab_re.reshape(1, S5_NS), ab_im.reshape(1, S5_NS), cmat.astype(BF16)


def _s5(u, tables, d_skip, w_glu, b_glu, nb, lp, q):
    bmat, a_re, a_im, cmat = tables
    blk = pl.BlockSpec((q * nb, S5_WIDTH), lambda c: (c, 0))
    return pl.pallas_call(
        functools.partial(_s5_kernel, nb=nb, q=q),
        grid=(lp // q,),
        in_specs=[blk,
                  _const_spec((S5_WIDTH, 2 * S5_NS)),
                  _const_spec((1, S5_NS)),
                  _const_spec((1, S5_NS)),
                  _const_spec((2 * S5_NS, S5_WIDTH)),
                  _const_spec((1, S5_WIDTH)),
                  _const_spec((S5_WIDTH, S5_WIDTH)),
                  _const_spec((1, S5_WIDTH))],
        out_specs=blk,
        out_shape=jax.ShapeDtypeStruct((lp * nb, S5_WIDTH), F32),
        scratch_shapes=[pltpu.VMEM((nb, 2 * S5_NS), F32),
                        pltpu.VMEM((nb * q, 2 * S5_NS), F32),
                        pltpu.VMEM((nb * q, 2 * S5_NS), F32)],
        compiler_params=_cparams("arbitrary"),
        name="s5",
    )(u, bmat, a_re, a_im, cmat, d_skip.reshape(1, S5_WIDTH).astype(F32), w_glu.astype(BF16),
      b_glu.reshape(1, S5_WIDTH).astype(F32))


SSD_PAIRS = SSD_HEADS // 2


def _ssd_tables():
    sel = np.zeros((LANES, SSD_INNER), np.float32)
    for h in range(SSD_HEADS):
        sel[h, h * SSD_HEADDIM:(h + 1) * SSD_HEADDIM] = 1.0
    return sel


def _ssd_kernel(z_ref, xbc_ref, dt_ref, cw_ref, cb_ref, dtb_ref, alog_ref, dsk_ref, nw_ref, sel_ref, o_ref,
                carry_ref, ext_ref, st_ref, *, bpb):
    c = pl.program_id(1)

    @pl.when(c == 0)
    def _():
        carry_ref[...] = jnp.zeros_like(carry_ref)
        st_ref[...] = jnp.zeros_like(st_ref)

    batches = range(bpb)
    x = xbc_ref[...]
    ext_ref[0:SUBLANES, :] = carry_ref[...]
    ext_ref[SUBLANES:SUBLANES + CHUNK, :] = x
    carry_ref[...] = x[CHUNK - SUBLANES:CHUNK, :]
    cw = cw_ref[...]
    conv = cb_ref[...] + cw[SSD_CONV - 1:SSD_CONV] * x
    for k in range(SSD_CONV - 1):
        off = SUBLANES - (SSD_CONV - 1) + k
        conv = conv + cw[k:k + 1] * ext_ref[off:off + CHUNK, :]
    xa = _silu(conv)

    row = lax.broadcasted_iota(jnp.int32, (CHUNK, 1), 0)
    dt = _softplus(dt_ref[...] + dtb_ref[...])
    dt = jnp.where(jnp.logical_or(c > 0, row >= PAD), dt, 0.0)
    dt_e = jnp.concatenate([_dot_sel_rhs(dt[:, b * LANES:(b + 1) * LANES], sel_ref[...]) for b in batches],
                           axis=1)
    d_e = dt_e * (-jnp.exp(alog_ref[...]))
    li = lax.broadcasted_iota(jnp.int32, (CHUNK, CHUNK), 0)
    si = lax.broadcasted_iota(jnp.int32, (CHUNK, CHUNK), 1)
    acum_e = _dot_sel_lhs((li >= si).astype(BF16), d_e)
    tri_t = (li <= si).astype(BF16)
    hrows = 2 * SUBLANES
    d_t = jnp.concatenate(
        [d_e[:, b * SSD_INNER + h * SSD_HEADDIM:b * SSD_INNER + h * SSD_HEADDIM + 1] if h < SSD_HEADS
         else jnp.zeros((CHUNK, 1), F32) for b in batches for h in range(hrows)], axis=1).T
    acum_t = _dot_sel_rhs(d_t, jnp.concatenate([tri_t, tri_t], axis=1))

    l2 = lax.broadcasted_iota(jnp.int32, (CHUNK, LANES), 0)
    lane = lax.broadcasted_iota(jnp.int32, (CHUNK, LANES), 1)
    low = lane < SSD_HEADDIM
    causal2 = l2 >= jnp.where(low, lane, lane - SSD_HEADDIM)

    groups = [(b, g) for b in batches for g in range(SSD_GROUPS)]
    chains = [(b, p) for b in batches for p in range(SSD_PAIRS)]
    gi = lambda b, p: b * SSD_GROUPS + p // (SSD_PAIRS // SSD_GROUPS)
    xcol = lambda b, off: slice(b * SSD_XBC + off, b * SSD_XBC + off + SSD_STATE)
    ycol = lambda b, p: slice(b * SSD_INNER + p * LANES, b * SSD_INNER + (p + 1) * LANES)
    bg = [xa[:, xcol(b, SSD_INNER + g * SSD_STATE)] for b, g in groups]
    cg = [xa[:, xcol(b, SSD_INNER + (SSD_GROUPS + g) * SSD_STATE)] for b, g in groups]
    cb2 = [_dot_nt(cg[i], jnp.concatenate([bg[i], bg[i]], axis=0)) for i in range(len(groups))]
    bg_t = [t.T for t in bg]
    xs = [xa[:, b * SSD_XBC + p * LANES:b * SSD_XBC + (p + 1) * LANES] for b, p in chains]
    ac = [acum_e[:, ycol(b, p)] for b, p in chains]
    a_last = [acum_e[CHUNK - 1:CHUNK, ycol(b, p)] for b, p in chains]
    ac_s = [jnp.where(low[0:1], acum_t[b * hrows + 2 * p:b * hrows + 2 * p + 1],
                      acum_t[b * hrows + 2 * p + 1:b * hrows + 2 * p + 2]) for b, p in chains]
    n = range(len(chains))
    m = [jnp.where(causal2, cb2[gi(*chains[i])] * jnp.exp(jnp.minimum(ac[i] - ac_s[i], 0.0)), 0.0) for i in n]
    xdt = [xs[i] * dt_e[:, ycol(*chains[i])] for i in n]
    xdt_bd = [jnp.concatenate([jnp.where(low, xdt[i], 0.0), jnp.where(low, 0.0, xdt[i])], axis=0)
              for i in n]
    st = [st_ref[i] for i in n]
    y_in = [_dot(m[i], xdt_bd[i]) for i in n]
    y_st = [_dot(cg[gi(*chains[i])], st[i]) for i in n]
    st_in = [_dot(bg_t[gi(*chains[i])], xdt[i] * jnp.exp(a_last[i] - ac[i])) for i in n]
    for i in n:
        st_ref[i] = st[i] * jnp.exp(a_last[i]) + st_in[i]
    y = jnp.concatenate([y_in[i] + y_st[i] * jnp.exp(ac[i]) for i in n], axis=1)
    xs_all = jnp.concatenate([xa[:, b * SSD_XBC:b * SSD_XBC + SSD_INNER] for b in batches], axis=1)
    y = (y + xs_all * dsk_ref[...]) * _silu(z_ref[...])
    gw = SSD_INNER // SSD_GROUPS
    for j in range(bpb * SSD_GROUPS):
        o_ref[:, j * gw:(j + 1) * gw] = _rms(y[:, j * gw:(j + 1) * gw], nw_ref[:, j * gw:(j + 1) * gw])


SSD_BPB = 4


def _ssd(z, xbc, dtr, conv_w, conv_b, dt_bias, a_log, d_skip, norm_w, nb, lp):
    bpb = SSD_BPB if nb % SSD_BPB == 0 else 1
    rep = lambda t: jnp.tile(t, (1, bpb))
    pad_h = lambda t: jnp.zeros((1, LANES), F32).at[0, :SSD_HEADS].set(t.astype(F32))
    per_ch = lambda t: jnp.repeat(t.astype(F32), SSD_HEADDIM).reshape(1, SSD_INNER)
    cw8 = jnp.zeros((SUBLANES, SSD_XBC), F32).at[:SSD_CONV].set(conv_w.astype(F32))
    spec = lambda width: pl.BlockSpec((CHUNK, bpb * width), lambda b, c: (c, b))
    out = pl.pallas_call(
        functools.partial(_ssd_kernel, bpb=bpb),
        grid=(nb // bpb, lp // CHUNK),
        in_specs=[spec(SSD_INNER), spec(SSD_XBC), spec(LANES),
                  _const_spec((SUBLANES, bpb * SSD_XBC)), _const_spec((1, bpb * SSD_XBC)),
                  _const_spec((1, bpb * LANES)), _const_spec((1, bpb * SSD_INNER)),
                  _const_spec((1, bpb * SSD_INNER)), _const_spec((1, bpb * SSD_INNER)),
                  _const_spec((LANES, SSD_INNER))],
        out_specs=spec(SSD_INNER),
        out_shape=jax.ShapeDtypeStruct((lp, nb * SSD_INNER), F32),
        scratch_shapes=[pltpu.VMEM((SUBLANES, bpb * SSD_XBC), F32),
                        pltpu.VMEM((CHUNK + SUBLANES, bpb * SSD_XBC), F32),
                        pltpu.VMEM((bpb * SSD_PAIRS, SSD_STATE, LANES), F32)],
        compiler_params=_cparams("arbitrary", "arbitrary"),
        name="ssd",
    )(z, xbc, dtr, rep(cw8), rep(conv_b.reshape(1, SSD_XBC).astype(F32)), rep(pad_h(dt_bias)),
      rep(per_ch(a_log)), rep(per_ch(d_skip)), rep(norm_w.reshape(1, SSD_INNER).astype(F32)),
      jnp.asarray(_ssd_tables(), BF16))
    return out


HG_LEVELS = 6


def _hgrn_tables():
    e = np.zeros((2 + HG_LEVELS, CHUNK, CHUNK), np.float32)
    m = np.zeros((1 + HG_LEVELS, CHUNK, CHUNK), np.float32)
    idx = np.arange(CHUNK)
    e[0] = idx[None, :] <= idx[:, None]
    e[1] = idx[None, :] > idx[:, None]
    m[0] = np.eye(CHUNK)
    for k in range(1, HG_LEVELS + 1):
        half = 1 << (k - 1)
        for r in range(CHUNK):
            bound = ((r >> k) << k) + half - 1
            if r > bound:
                e[1 + k, r, bound + 1:r + 1] = 1.0
            else:
                e[1 + k, r, r + 1:bound + 1] = 1.0
        same = (idx[:, None] >> k) == (idx[None, :] >> k)
        upper = ((idx[:, None] >> (k - 1)) & 1) == 1
        lower = ((idx[None, :] >> (k - 1)) & 1) == 0
        m[k] = same & upper & lower
    return e.reshape(-1, CHUNK), m


def _hgrn_kernel(p_ref, lb_ref, nw_ref, e_ref, m_ref, o_ref, st_ref, *, bpb):
    @pl.when(pl.program_id(1) == 0)
    def _():
        st_ref[...] = jnp.zeros_like(st_ref)

    hd, wd = HGRN_HEADDIM, HGRN_WIDTH
    chains = range(bpb * HGRN_HEADS)
    part = lambda i: jnp.concatenate([p_ref[:, b * HGRN_IN + i * wd:b * HGRN_IN + (i + 1) * wd]
                                      for b in range(bpb)], axis=1)
    tile = lambda ref: jnp.concatenate([ref[...]] * bpb, axis=1)
    cs = lambda x, c: x[:, c * hd:(c + 1) * hd]
    q = _silu(part(0))
    lb = tile(lb_ref)
    forget = lb + (1.0 - lb) * _sigmoid(part(1))
    kf = 1.0 - forget
    eg = jnp.exp(_dot_sel_lhs(e_ref[...], jnp.log(forget)))
    att = [m_ref[0] * _dot_nt(cs(q, c), cs(kf, c)) for c in chains]
    for k in range(1, HG_LEVELS + 1):
        ek = eg[(1 + k) * CHUNK:(2 + k) * CHUNK]
        qe, ke = q * ek, kf * ek
        att = [att[c] + m_ref[k] * _dot_nt(cs(qe, c), cs(ke, c)) for c in chains]
    qg = q * eg[0:CHUNK]
    kg = kf * eg[CHUNK:2 * CHUNK]
    iv = part(2)
    st = [st_ref[c] for c in chains]
    out = [_dot(att[c], cs(iv, c)) + _dot_nt(cs(qg, c), st[c]) for c in chains]
    for c in chains:
        st_ref[c] = st[c] * cs(eg[CHUNK - 1:CHUNK], c) + _dot_tn(cs(iv, c), cs(kg, c))
    og = _silu(part(3))
    nw = tile(nw_ref)
    for c in chains:
        o_ref[:, c * hd:(c + 1) * hd] = _rms(out[c], cs(nw, c)) * cs(og, c)


HGRN_BPB = 4


def _hgrn(p, lb, norm_w, nb, lp):
    bpb = HGRN_BPB if nb % HGRN_BPB == 0 else 1
    e_np, m_np = _hgrn_tables()
    spec = lambda width: pl.BlockSpec((CHUNK, bpb * width), lambda b, c: (c, b))
    out = pl.pallas_call(
        functools.partial(_hgrn_kernel, bpb=bpb),
        grid=(nb // bpb, lp // CHUNK),
        in_specs=[spec(HGRN_IN), _const_spec((1, HGRN_WIDTH)), _const_spec((1, HGRN_WIDTH)),
                  _const_spec(e_np.shape), _const_spec(m_np.shape)],
        out_specs=spec(HGRN_WIDTH),
        out_shape=jax.ShapeDtypeStruct((lp, nb * HGRN_WIDTH), F32),
        scratch_shapes=[pltpu.VMEM((bpb * HGRN_HEADS, HGRN_HEADDIM, HGRN_HEADDIM), F32)],
        compiler_params=_cparams("arbitrary", "arbitrary"),
        name="hgrn2",
    )(p, lb.reshape(1, HGRN_WIDTH).astype(F32),
      norm_w.reshape(1, HGRN_WIDTH).astype(F32), jnp.asarray(e_np, BF16), jnp.asarray(m_np, F32))
    return out


def _head_ones():
    idx = np.arange(RWKV_WIDTH) // RWKV_HEADDIM
    return (idx[:, None] == idx[None, :]).astype(np.float32)


def _rwkv_prep_kernel(*refs, tm, nb, has_vres):
    (p_ref, mu_ref, w0_ref, w2_ref, a0_ref, a2_ref, g2_ref, kk_ref, ka_ref, rk_ref, ones_ref) = refs[:11]
    if has_vres:
        vf_ref, v0_ref, v2_ref = refs[11:14]
        kkb_o, wk_o, rv_o, g_o, bv_o, carry_ref, ext_ref = refs[14:]
    else:
        kkb_o, wk_o, rv_o, g_o, bv_o, v_o, carry_ref, ext_ref = refs[11:]

    @pl.when(pl.program_id(0) == 0)
    def _():
        carry_ref[...] = jnp.zeros_like(carry_ref)

    p = p_ref[...]
    ext_ref[0:nb, :] = carry_ref[...]
    ext_ref[nb:nb + tm, :] = p
    carry_ref[...] = p[tm - nb:tm, :]
    prev = ext_ref[0:tm, :]
    ps = p + (prev - p) * mu_ref[...]

    wd = RWKV_WIDTH
    r, k, v = ps[:, :wd], ps[:, wd:2 * wd], ps[:, 2 * wd:3 * wd]
    pwa = ps[:, 3 * wd:3 * wd + LORA_W + LORA_A]
    pg = ps[:, 3 * wd + LORA_W + LORA_A:RWKV_IN]
    w_log = -_softplus(-(w0_ref[...] + _dot(jnp.tanh(pwa), w2_ref[...]))) - 0.5
    w = jnp.exp(-jnp.exp(w_log))
    a = _sigmoid(a0_ref[...] + _dot(pwa, a2_ref[...]))
    if has_vres:
        pv = ps[:, RWKV_IN:RWKV_INP]
        v = v + (vf_ref[...] - v) * _sigmoid(v0_ref[...] + _dot(pv, v2_ref[...]))
    else:
        v_o[...] = v
    g_o[...] = _dot(_sigmoid(pg), g2_ref[...])
    kk = k * kk_ref[...]
    ss = _dot_sel_rhs(kk * kk, ones_ref[...])
    kk = kk * lax.rsqrt(jnp.maximum(ss, 1e-24))
    k2 = k * (1.0 + (a - 1.0) * ka_ref[...])
    bonus = _dot_sel_rhs(r * k2 * rk_ref[...], ones_ref[...])
    bv_o[...] = bonus * v
    hd = RWKV_HEADDIM
    for o_ref, first, second in ((kkb_o, kk, kk * a), (wk_o, w, k2), (rv_o, r, v)):
        for h in range(RWKV_HEADS):
            both = jnp.concatenate([first[:, h * hd:(h + 1) * hd], second[:, h * hd:(h + 1) * hd]], axis=1)
            o_ref[:, h] = both.reshape(tm // nb, nb, 2 * hd)


def _rwkv_prep(p, mu, w0, w2, a0, a2, g2, k_k, k_a, r_k, vres, tm, nb):
    rows = p.shape[0]
    wd = RWKV_WIDTH
    row = lambda t: t.reshape(1, -1).astype(F32)
    w2p = jnp.zeros((LORA_W + LORA_A, wd), F32).at[:LORA_W].set(w2).astype(BF16)
    a2p = jnp.zeros((LORA_W + LORA_A, wd), F32).at[LORA_W:].set(a2).astype(BF16)
    args = [p, row(mu), row(w0), w2p, row(a0), a2p, g2.astype(BF16), row(k_k), row(k_a), row(r_k),
            jnp.asarray(_head_ones(), BF16)]
    specs = [pl.BlockSpec((tm, RWKV_INP), lambda i: (i, 0)), _const_spec((1, RWKV_INP)),
             _const_spec((1, wd)), _const_spec((LORA_W + LORA_A, wd)), _const_spec((1, wd)),
             _const_spec((LORA_W + LORA_A, wd)), _const_spec((LORA_G, wd)), _const_spec((1, wd)),
             _const_spec((1, wd)), _const_spec((1, wd)), _const_spec((wd, wd))]
    if vres is not None:
        v_first, v0, v2 = vres
        v2p = jnp.zeros((RWKV_INP - RWKV_IN, wd), F32).at[:LORA_V].set(v2).astype(BF16)
        args += [v_first, row(v0), v2p]
        specs += [pl.BlockSpec((tm, wd), lambda i: (i, 0)), _const_spec((1, wd)),
                  _const_spec((RWKV_INP - RWKV_IN, wd))]
    pair_shape = (rows // nb, RWKV_HEADS, nb, 2 * RWKV_HEADDIM)
    pair_spec = pl.BlockSpec((tm // nb,) + pair_shape[1:], lambda i: (i, 0, 0, 0))
    row_spec = pl.BlockSpec((tm, wd), lambda i: (i, 0))
    row_shape = jax.ShapeDtypeStruct((rows, wd), F32)
    n_row = 2 if vres is not None else 3
    return pl.pallas_call(
        functools.partial(_rwkv_prep_kernel, tm=tm, nb=nb, has_vres=vres is not None),
        grid=(rows // tm,),
        in_specs=specs,
        out_specs=[pair_spec] * 3 + [row_spec] * n_row,
        out_shape=[jax.ShapeDtypeStruct(pair_shape, F32)] * 3 + [row_shape] * n_row,
        scratch_shapes=[pltpu.VMEM((nb, RWKV_INP), F32),
                        pltpu.VMEM((tm + nb, RWKV_INP), F32)],
        compiler_params=_cparams("arbitrary"),
        name="rwkv_prep",
    )(*args)


RW_VLO = RWKV_HEADDIM // 2
RW_ACC = 4


RW_LAG = 4


def _rwkv_scan_kernel(kkb_ref, wk_ref, rv_ref, o_ref,
                      m_ref, rs_ref, ws_ref, ks_ref, kks_ref, bs_ref, vs_ref, os_ref, *, tb, ni):
    @pl.when(pl.program_id(0) == 0)
    def _():
        m_ref[...] = jnp.zeros_like(m_ref)

    hd = RWKV_HEADDIM
    nl = 2 * ni
    shape = (RW_VLO, nl)

    low_lanes = lax.broadcasted_iota(jnp.int32, shape, 1) < ni

    def load(t):
        for src, first, second in ((kkb_ref, kks_ref, bs_ref), (wk_ref, ws_ref, ks_ref), (rv_ref, rs_ref, None)):
            x = src[t].reshape(ni, 2 * hd)
            tt = jnp.concatenate([x, x], axis=0).T
            first[t] = tt[:hd]
            if second is not None:
                second[t] = tt[hd:]
            else:
                vs_ref[t] = jnp.where(low_lanes, tt[hd:hd + RW_VLO], tt[hd + RW_VLO:])

    def tree(parts):
        while len(parts) > 1:
            parts = [parts[i] + parts[i + 1] for i in range(0, len(parts), 2)]
        return parts[0]

    def step(t):
        vt = vs_ref[t]
        acc = [None] * RW_ACC
        for k in range(hd):
            term = m_ref[k] * jnp.broadcast_to(kks_ref[t, k:k + 1, :], shape)
            acc[k % RW_ACC] = term if acc[k % RW_ACC] is None else acc[k % RW_ACC] + term
        sa = tree(acc)
        acc = [None] * RW_ACC
        for k in range(hd):
            bc = lambda ref: jnp.broadcast_to(ref[t, k:k + 1, :], shape)
            mk = m_ref[k] * bc(ws_ref) - sa * bc(bs_ref) + vt * bc(ks_ref)
            m_ref[k] = mk
            term = mk * bc(rs_ref)
            acc[k % RW_ACC] = term if acc[k % RW_ACC] is None else acc[k % RW_ACC] + term
        os_ref[t] = tree(acc)

    def store(t):
        o = os_ref[t]
        o_ref[t] = jnp.concatenate([o[:, :ni], o[:, ni:]], axis=0).T.reshape(o_ref.shape[1:])

    def pair(t0, do_load, do_store):
        step(t0)
        step(t0 + 1)
        if do_load:
            load(t0 + RW_LAG)
            load(t0 + RW_LAG + 1)
        if do_store:
            store(t0 - 2)
            store(t0 - 1)

    for t in range(RW_LAG):
        load(t)
    pair(0, True, False)

    def body(i, carry):
        pair(2 * i, True, True)
        return carry

    n_pairs = tb // 2
    lax.fori_loop(1, n_pairs - RW_LAG // 2, body, 0)
    for i in range(n_pairs - RW_LAG // 2, n_pairs):
        pair(2 * i, False, True)
    store(tb - 2)
    store(tb - 1)


def _rwkv_scan(kkb, wk, rv, nb, lp, tb):
    nh, hd = RWKV_HEADS, RWKV_HEADDIM
    ni = nb * nh
    nl = 2 * ni
    in_spec = pl.BlockSpec((tb, nh, nb, 2 * hd), lambda i: (i, 0, 0, 0))
    return pl.pallas_call(
        functools.partial(_rwkv_scan_kernel, tb=tb, ni=ni),
        grid=(lp // tb,),
        in_specs=[in_spec] * 3,
        out_specs=pl.BlockSpec((tb, nh, nb, hd), lambda i: (i, 0, 0, 0)),
        out_shape=jax.ShapeDtypeStruct((lp, nh, nb, hd), F32),
        scratch_shapes=[pltpu.VMEM((hd, RW_VLO, nl), F32)] + [pltpu.VMEM((tb, hd, nl), F32)] * 5
                       + [pltpu.VMEM((tb, RW_VLO, nl), F32)] * 2,
        compiler_params=_cparams("arbitrary"),
        name="rwkv_scan",
    )(kkb, wk, rv)


def _rwkv_post_kernel(o_ref, bv_ref, g_ref, lw_ref, lb_ref, ones_ref, y_ref, *, tm):
    o = jnp.concatenate([o_ref[:, h].reshape(tm, RWKV_HEADDIM) for h in range(RWKV_HEADS)], axis=1)
    inv = 1.0 / RWKV_HEADDIM
    mean = _dot_sel_rhs(o, ones_ref[...]) * inv
    xc = o - mean
    var = _dot_sel_rhs(xc * xc, ones_ref[...]) * inv
    y = xc * lax.rsqrt(var + GN_EPS) * lw_ref[...] + lb_ref[...] + bv_ref[...]
    y_ref[...] = y * g_ref[...]


def _rwkv_post(o, bv, g, ln_w, ln_b, tm, nb):
    rows = bv.shape[0]
    wd = RWKV_WIDTH
    blk = pl.BlockSpec((tm, wd), lambda i: (i, 0))
    return pl.pallas_call(
        functools.partial(_rwkv_post_kernel, tm=tm),
        grid=(rows // tm,),
        in_specs=[pl.BlockSpec((tm // nb, RWKV_HEADS, nb, RWKV_HEADDIM), lambda i: (i, 0, 0, 0)),
                  blk, blk, _const_spec((1, wd)), _const_spec((1, wd)), _const_spec((wd, wd))],
        out_specs=blk,
        out_shape=jax.ShapeDtypeStruct((rows, wd), F32),
        compiler_params=_cparams("arbitrary"),
        name="rwkv_post",
    )(o, bv, g, ln_w.reshape(1, wd).astype(F32), ln_b.reshape(1, wd).astype(F32),
      jnp.asarray(_head_ones(), BF16))


def _even_weight(w_in):
    padw = jnp.zeros((D_MODEL, LANES - SSD_HEADS), w_in.dtype)
    return jnp.concatenate([w_in, padw], axis=1).astype(BF16)


def kernel(x, meta, norm_mix_pre, norm_mix_post, norm_ffn_pre, norm_ffn_post, mix_w_out, ffn_w_up, ffn_conv_w, ffn_conv_b, ffn_w_down, ev_w_in, s5_lam_re, s5_lam_im, s5_log_dt, s5_b_re, s5_b_im, s5_c_re, s5_c_im, s5_d, s5_w_glu, s5_b_glu, ssd_conv_w, ssd_conv_b, ssd_dt_bias, ssd_a_log, ssd_d, ssd_norm, od_w_in, rw_mu, rw_w0, rw_w2, rw_a0, rw_a2, rw_g2, rw_k_k, rw_k_a, rw_r_k, rw_ln_w, rw_ln_b, rw_w_vin, rw_mu_v, rw_v0, rw_v2, hg_lb_raw, hg_norm):
    nb, seq, _ = x.shape
    depth = norm_mix_pre.shape[0]
    lp = PAD + N_META + seq
    rows = lp * nb
    tm = _row_tile(rows, 512)
    h = jnp.concatenate([jnp.zeros((PAD, nb, D_MODEL), x.dtype),
                         jnp.broadcast_to(meta.astype(x.dtype)[:, None], (N_META, nb, D_MODEL)),
                         x.transpose(1, 0, 2)], axis=0)
    h = h.reshape(rows, D_MODEL)

    lb_w = jax.nn.softmax(hg_lb_raw.astype(F32), axis=0)
    lb_table = jnp.cumsum(lb_w, axis=0) - lb_w[0]
    v_first = None
    s1 = S5_WIDTH
    s2 = s1 + SSD_INNER
    s3 = s2 + SSD_XBC
    for layer in range(depth):
        if layer % 2 == 0:
            e = layer // 2
            u, z, xbc, dtr = _norm_proj(h, norm_mix_pre[layer], _even_weight(ev_w_in[e]),
                                        ((0, s1, False), (s1, SSD_INNER, True), (s2, SSD_XBC, True),
                                         (s3, LANES, True)), tm, nb)
            tables = _s5_tables(s5_lam_re[e], s5_lam_im[e], s5_log_dt[e], s5_b_re[e], s5_b_im[e],
                                s5_c_re[e], s5_c_im[e])
            y_a = _s5(u, tables, s5_d[e], s5_w_glu[e], s5_b_glu[e], nb, lp, CHUNK)
            y_b = _ssd(z, xbc, dtr, ssd_conv_w[e], ssd_conv_b[e], ssd_dt_bias[e], ssd_a_log[e], ssd_d[e],
                       ssd_norm[e], nb, lp)
        else:
            o = layer // 2
            mu = jnp.zeros((RWKV_INP,), F32).at[:RWKV_IN].set(rw_mu[o])
            w_rw = jnp.zeros((D_MODEL, RWKV_INP), F32).at[:, :RWKV_IN].set(od_w_in[o][:, :RWKV_IN])
            if o > 0:
                mu = mu.at[RWKV_IN:RWKV_IN + LORA_V].set(rw_mu_v[o - 1])
                w_rw = w_rw.at[:, RWKV_IN:RWKV_IN + LORA_V].set(rw_w_vin[o - 1])
            w_in = jnp.concatenate([w_rw, od_w_in[o][:, RWKV_IN:]], axis=1).astype(BF16)
            p_rw, p_hg = _norm_proj(h, norm_mix_pre[layer], w_in,
                                    ((0, RWKV_INP, False), (RWKV_INP, HGRN_IN, True)), tm, nb)
            vres = None if o == 0 else (v_first, rw_v0[o - 1], rw_v2[o - 1])
            outs = _rwkv_prep(p_rw, mu, rw_w0[o], rw_w2[o], rw_a0[o], rw_a2[o], rw_g2[o], rw_k_k[o],
                              rw_k_a[o], rw_r_k[o], vres, tm, nb)
            kkb, wk, rv, g, bv = outs[:5]
            if o == 0:
                v_first = outs[5]
            o_rw = _rwkv_scan(kkb, wk, rv, nb, lp, 32)
            y_a = _rwkv_post(o_rw, bv, g, rw_ln_w[o], rw_ln_b[o], tm, nb)
            y_b = _hgrn(p_hg, lb_table[o], hg_norm[o], nb, lp)
        ka = y_a.shape[1]
        w_out = mix_w_out[layer].astype(BF16)
        h = _out_proj(y_a, y_b, w_out[:ka], w_out[ka:], norm_mix_post[layer], h, tm, nb)
        h = _ffn(h, norm_ffn_pre[layer], ffn_w_up[layer].astype(BF16), ffn_conv_w[layer], ffn_conv_b[layer],
                 ffn_w_down[layer].astype(BF16), norm_ffn_post[layer], tm, nb)
    return h.reshape(lp, nb, D_MODEL)[PAD + N_META:].transpose(1, 0, 2)
```

```python
import functools

import numpy as np
import jax
import jax.numpy as jnp
from jax import lax
from jax.experimental import pallas as pl
from jax.experimental.pallas import tpu as pltpu

F32 = jnp.float32
BF16 = jnp.bfloat16

D_MODEL = 1024
N_META = 16
CHUNK = 64
PAD = CHUNK - N_META
RMS_EPS = 1e-6

S5_WIDTH = 256
S5_GROUP = 16
S5_GROUPS = S5_WIDTH // S5_GROUP
S5_STATE = 64
S5_NS = S5_GROUPS * S5_STATE

SSD_HEADDIM = 64
SSD_INNER = 768
SSD_HEADS = SSD_INNER // SSD_HEADDIM
SSD_GROUPS = 2
SSD_HPG = SSD_HEADS // SSD_GROUPS
SSD_STATE = 128
SSD_CONV = 4
SSD_XBC = SSD_INNER + 2 * SSD_GROUPS * SSD_STATE
MIX_WIDTH = S5_WIDTH + SSD_INNER

RWKV_WIDTH = 512
RWKV_HEADDIM = 64
RWKV_HEADS = RWKV_WIDTH // RWKV_HEADDIM
LORA_W = 64
LORA_A = 64
LORA_V = 32
LORA_G = 128
GN_EPS = 64e-5
RWKV_IN = 3 * RWKV_WIDTH + LORA_W + LORA_A + LORA_G
RWKV_INP = RWKV_IN + 128

HGRN_WIDTH = 512
HGRN_HEADS = 4
HGRN_HEADDIM = HGRN_WIDTH // HGRN_HEADS
HGRN_IN = 4 * HGRN_WIDTH

D_FF = 2816
FFN_CONV = 3

LANES = 128
SUBLANES = 8
VMEM_LIMIT = 56 * 1024 * 1024


def _cparams(*sem):
    return pltpu.CompilerParams(dimension_semantics=sem, vmem_limit_bytes=VMEM_LIMIT)


def _row_tile(rows, target):
    best = SUBLANES
    for t in range(SUBLANES, min(rows, target) + 1, SUBLANES):
        if rows % t == 0:
            best = t
    return best


def _const_spec(shape):
    nd = len(shape)
    return pl.BlockSpec(shape, lambda *_: (0,) * nd, pipeline_mode=pl.Buffered(1))


def _dot(a, b):
    return jnp.dot(a.astype(BF16), b.astype(BF16), preferred_element_type=F32)


def _dot_nt(a, b):
    return lax.dot_general(a.astype(BF16), b.astype(BF16), (((1,), (1,)), ((), ())),
                           preferred_element_type=F32)


def _dot_tn(a, b):
    return lax.dot_general(a.astype(BF16), b.astype(BF16), (((0,), (0,)), ((), ())),
                           preferred_element_type=F32)


def _split3(x):
    hi = x.astype(BF16)
    r1 = x - hi.astype(F32)
    mid = r1.astype(BF16)
    lo = (r1 - mid.astype(F32)).astype(BF16)
    return hi, mid, lo


def _dot_sel_lhs(sel, x):
    hi, mid, lo = _split3(x)
    d = lambda p: jnp.dot(sel, p, preferred_element_type=F32)
    return d(hi) + d(mid) + d(lo)


def _dot_sel_rhs(x, sel):
    hi, mid, lo = _split3(x)
    d = lambda p: jnp.dot(p, sel, preferred_element_type=F32)
    return d(hi) + d(mid) + d(lo)


def _sigmoid(x):
    return 1.0 / (1.0 + jnp.exp(-x))


def _silu(x):
    return x * _sigmoid(x)


def _softplus(x):
    return jnp.maximum(x, 0.0) + jnp.log(1.0 + jnp.exp(-jnp.abs(x)))


GELU_C = 0.7978845608028654
GELU_A = 0.044715


def _gelu_tanh(x):
    return 0.5 * x * (1.0 + jnp.tanh(GELU_C * (x + GELU_A * (x * x * x))))


def _rms(x, g):
    return x * lax.rsqrt(jnp.mean(x * x, axis=-1, keepdims=True) + RMS_EPS) * g


def _keep_rows(tm, nb):
    row = pl.program_id(0) * tm + lax.broadcasted_iota(jnp.int32, (tm, 1), 0)
    return row >= PAD * nb


def _norm_proj_kernel(x_ref, g_ref, w_ref, *refs, splits, chunk, nb):
    o_refs, tiles_ref = refs[:len(splits)], refs[len(splits)]
    tm = x_ref.shape[0]
    hn = _rms(x_ref[...], g_ref[...]).astype(BF16)
    for o_ref, (start, width, by_batch) in zip(o_refs, splits):
        for c0 in range(0, width, chunk):
            cw = min(chunk, width - c0)
            res = jnp.dot(hn, w_ref[:, start + c0:start + c0 + cw], preferred_element_type=F32)
            if not by_batch:
                o_ref[:, c0:c0 + cw] = res
            else:
                for j in range(cw // LANES):
                    tiles_ref[c0 // LANES + j] = res[:, j * LANES:(j + 1) * LANES]
        if by_batch:
            for j in range(width // LANES):
                for b in range(nb):
                    o_ref[:, b * width + j * LANES:b * width + (j + 1) * LANES] = (
                        tiles_ref[j, pl.ds(b, tm // nb, stride=nb), :])


def _norm_proj(h, g, w, splits, tm, nb):
    rows = h.shape[0]
    n = w.shape[1]
    spec = lambda wd, bb: pl.BlockSpec((tm // nb, nb * wd) if bb else (tm, wd), lambda i: (i, 0))
    shape = lambda wd, bb: jax.ShapeDtypeStruct((rows // nb, nb * wd) if bb else (rows, wd), F32)
    max_w = max([wd for _, wd, bb in splits if bb] + [LANES])
    return pl.pallas_call(
        functools.partial(_norm_proj_kernel, splits=splits, chunk=512, nb=nb),
        grid=(rows // tm,),
        in_specs=[pl.BlockSpec((tm, D_MODEL), lambda i: (i, 0)),
                  _const_spec((1, D_MODEL)),
                  _const_spec((D_MODEL, n))],
        out_specs=[spec(wd, bb) for _, wd, bb in splits],
        out_shape=[shape(wd, bb) for _, wd, bb in splits],
        scratch_shapes=[pltpu.VMEM((max_w // LANES, tm, LANES), F32)],
        compiler_params=_cparams("arbitrary"),
        name="norm_proj",
    )(h, g.reshape(1, D_MODEL), w)


def _out_proj_kernel(ya_ref, yb_ref, wa_ref, wb_ref, g_ref, h_ref, o_ref, tiles_ref, *, tm, nb, kb):
    for j in range(kb // LANES):
        for b in range(nb):
            tiles_ref[j, pl.ds(b, tm // nb, stride=nb), :] = yb_ref[:, b * kb + j * LANES:b * kb + (j + 1) * LANES]
    yb = jnp.concatenate([tiles_ref[j] for j in range(kb // LANES)], axis=1)
    o = _dot(ya_ref[...], wa_ref[...]) + _dot(yb, wb_ref[...])
    upd = _rms(o, g_ref[...])
    o_ref[...] = h_ref[...] + jnp.where(_keep_rows(tm, nb), upd, 0.0)


def _out_proj(ya, yb, wa, wb, g, h, tm, nb):
    rows = h.shape[0]
    ka, kb = ya.shape[1], yb.shape[1] // nb
    return pl.pallas_call(
        functools.partial(_out_proj_kernel, tm=tm, nb=nb, kb=kb),
        grid=(rows // tm,),
        in_specs=[pl.BlockSpec((tm, ka), lambda i: (i, 0)),
                  pl.BlockSpec((tm // nb, nb * kb), lambda i: (i, 0)),
                  _const_spec((ka, D_MODEL)),
                  _const_spec((kb, D_MODEL)),
                  _const_spec((1, D_MODEL)),
                  pl.BlockSpec((tm, D_MODEL), lambda i: (i, 0))],
        out_specs=pl.BlockSpec((tm, D_MODEL), lambda i: (i, 0)),
        out_shape=jax.ShapeDtypeStruct((rows, D_MODEL), F32),
        scratch_shapes=[pltpu.VMEM((kb // LANES, tm, LANES), F32)],
        compiler_params=_cparams("arbitrary"),
        name="out_proj",
    )(ya, yb, wa, wb, g.reshape(1, D_MODEL), h)


def _ffn_kernel(h_ref, g1_ref, wup_ref, cw_ref, cb_ref, wdn_ref, g2_ref, o_ref, carry_ref, act_ref,
                *, tm, tf, nb):
    halo = (FFN_CONV - 1) * nb

    @pl.when(pl.program_id(0) == 0)
    def _():
        carry_ref[...] = jnp.zeros_like(carry_ref)

    x = h_ref[...]
    hn = _rms(x, g1_ref[...]).astype(BF16)
    for c in range(D_FF // tf):
        halves = []
        for part in range(2):
            col = part * D_FF + c * tf
            u = jnp.dot(hn, wup_ref[:, col:col + tf], preferred_element_type=F32)
            hist = carry_ref[:, col:col + tf]
            carry_ref[:, col:col + tf] = u[tm - halo:tm, :]
            prev2 = jnp.concatenate([hist, u[:tm - halo, :]], axis=0)
            prev1 = jnp.concatenate([hist[nb:, :], u[:tm - nb, :]], axis=0)
            cw = cw_ref[:, col:col + tf]
            halves.append(cw[0:1] * prev2 + cw[1:2] * prev1 + cw[2:3] * u + cb_ref[:, col:col + tf])
        gate, val = halves
        th = jnp.tanh(gate * (GELU_C + (GELU_C * GELU_A) * (gate * gate)))
        act_ref[:, c * tf:(c + 1) * tf] = (gate * (0.5 + 0.5 * th) * val).astype(BF16)
    acc = jnp.dot(act_ref[...], wdn_ref[...], preferred_element_type=F32)
    upd = _rms(acc, g2_ref[...])
    o_ref[...] = x + jnp.where(_keep_rows(tm, nb), upd, 0.0)


def _ffn(h, g1, wup, cw, cb, wdn, g2, tm, nb):
    rows = h.shape[0]
    tf = 256
    halo = (FFN_CONV - 1) * nb
    cw8 = jnp.zeros((SUBLANES, 2 * D_FF), F32).at[:FFN_CONV].set(cw)
    return pl.pallas_call(
        functools.partial(_ffn_kernel, tm=tm, tf=tf, nb=nb),
        grid=(rows // tm,),
        in_specs=[pl.BlockSpec((tm, D_MODEL), lambda i: (i, 0)),
                  _const_spec((1, D_MODEL)),
                  _const_spec((D_MODEL, 2 * D_FF)),
                  _const_spec((SUBLANES, 2 * D_FF)),
                  _const_spec((1, 2 * D_FF)),
                  _const_spec((D_FF, D_MODEL)),
                  _const_spec((1, D_MODEL))],
        out_specs=pl.BlockSpec((tm, D_MODEL), lambda i: (i, 0)),
        out_shape=jax.ShapeDtypeStruct((rows, D_MODEL), F32),
        scratch_shapes=[pltpu.VMEM((halo, 2 * D_FF), F32),
                        pltpu.VMEM((tm, D_FF), BF16)],
        compiler_params=_cparams("arbitrary"),
        name="conv_ffn",
    )(h, g1.reshape(1, D_MODEL), wup, cw8, cb.reshape(1, 2 * D_FF), wdn, g2.reshape(1, D_MODEL))


def _s5_kernel(u_ref, bmat_ref, are_ref, aim_ref, cmat_ref, d_ref, wg_ref, bg_ref, o_ref,
               h_ref, bu_ref, hs_ref, *, nb, q):
    @pl.when(pl.program_id(0) == 0)
    def _():
        h_ref[...] = jnp.zeros_like(h_ref)

    u = u_ref[...]
    bu_ref[...] = _dot(u, bmat_ref[...])
    a_re = jnp.broadcast_to(are_ref[...], (nb, S5_NS))
    a_im = jnp.broadcast_to(aim_ref[...], (nb, S5_NS))

    def step(t, carry):
        h_re, h_im = carry
        rows = pl.ds(pl.multiple_of(t * nb, nb), nb)
        n_re = a_re * h_re - a_im * h_im + bu_ref[rows, :S5_NS]
        n_im = a_re * h_im + a_im * h_re + bu_ref[rows, S5_NS:]
        hs_ref[rows, :S5_NS] = n_re
        hs_ref[rows, S5_NS:] = n_im
        return n_re, n_im

    h_re, h_im = lax.fori_loop(0, q, step, (h_ref[:, :S5_NS], h_ref[:, S5_NS:]), unroll=4)
    h_ref[:, :S5_NS] = h_re
    h_ref[:, S5_NS:] = h_im
    y = _gelu_tanh(_dot(hs_ref[...], cmat_ref[...]) + d_ref[...] * u)
    o_ref[...] = y * _sigmoid(_dot(y, wg_ref[...]) + bg_ref[...])


def _s5_tables(lam_re, lam_im, log_dt, b_re, b_im, c_re, c_im):
    lr, li = lam_re.astype(F32), lam_im.astype(F32)
    dt = jnp.exp(log_dt.astype(F32))[:, None]
    mag = jnp.exp(lr * dt)
    ab_re, ab_im = mag * jnp.cos(li * dt), mag * jnp.sin(li * dt)
    den = lr * lr + li * li
    zr, zi = ab_re - 1.0, ab_im
    f_re = (zr * lr + zi * li) / den
    f_im = (zi * lr - zr * li) / den
    br, bi = b_re.astype(F32), b_im.astype(F32)
    bb_re = f_re[..., None] * br - f_im[..., None] * bi
    bb_im = f_re[..., None] * bi + f_im[..., None] * br
    eye = jnp.eye(S5_GROUPS, dtype=F32)
    to_b = lambda t: jnp.einsum("gpc,gh->gchp", t, eye).reshape(S5_WIDTH, S5_NS)
    bmat = jnp.concatenate([to_b(bb_re), to_b(bb_im)], axis=1)
    to_c = lambda t: jnp.einsum("gcp,gh->gphc", t.astype(F32), eye).reshape(S5_NS, S5_WIDTH)
    cmat = jnp.concatenate([to_c(c_re), -to_c(c_im)], axis=0)
    return bmat.astype(BF16), ab_re.reshape(1, S5_NS), ab_im.reshape(1, S5_NS), cmat.astype(BF16)


def _s5(u, tables, d_skip, w_glu, b_glu, nb, lp, q):
    bmat, a_re, a_im, cmat = tables
    blk = pl.BlockSpec((q * nb, S5_WIDTH), lambda c: (c, 0))
    return pl.pallas_call(
        functools.partial(_s5_kernel, nb=nb, q=q),
        grid=(lp // q,),
        in_specs=[blk,
                  _const_spec((S5_WIDTH, 2 * S5_NS)),
                  _const_spec((1, S5_NS)),
                  _const_spec((1, S5_NS)),
                  _const_spec((2 * S5_NS, S5_WIDTH)),
                  _const_spec((1, S5_WIDTH)),
                  _const_spec((S5_WIDTH, S5_WIDTH)),
                  _const_spec((1, S5_WIDTH))],
        out_specs=blk,
        out_shape=jax.ShapeDtypeStruct((lp * nb, S5_WIDTH), F32),
        scratch_shapes=[pltpu.VMEM((nb, 2 * S5_NS), F32),
                        pltpu.VMEM((nb * q, 2 * S5_NS), F32),
                        pltpu.VMEM((nb * q, 2 * S5_NS), F32)],
        compiler_params=_cparams("arbitrary"),
        name="s5",
    )(u, bmat, a_re, a_im, cmat, d_skip.reshape(1, S5_WIDTH).astype(F32), w_glu.astype(BF16),
      b_glu.reshape(1, S5_WIDTH).astype(F32))


SSD_PAIRS = SSD_HEADS // 2


def _ssd_tables():
    sel = np.zeros((LANES, SSD_INNER), np.float32)
    for h in range(SSD_HEADS):
        sel[h, h * SSD_HEADDIM:(h + 1) * SSD_HEADDIM] = 1.0
    return sel


def _ssd_kernel(z_ref, xbc_ref, dt_ref, cw_ref, cb_ref, dtb_ref, alog_ref, dsk_ref, nw_ref, sel_ref, o_ref,
                carry_ref, ext_ref, st_ref, *, bpb):
    c = pl.program_id(1)

    @pl.when(c == 0)
    def _():
        carry_ref[...] = jnp.zeros_like(carry_ref)
        st_ref[...] = jnp.zeros_like(st_ref)

    batches = range(bpb)
    x = xbc_ref[...]
    ext_ref[0:SUBLANES, :] = carry_ref[...]
    ext_ref[SUBLANES:SUBLANES + CHUNK, :] = x
    carry_ref[...] = x[CHUNK - SUBLANES:CHUNK, :]
    cw = cw_ref[...]
    conv = cb_ref[...] + cw[SSD_CONV - 1:SSD_CONV] * x
    for k in range(SSD_CONV - 1):
        off = SUBLANES - (SSD_CONV - 1) + k
        conv = conv + cw[k:k + 1] * ext_ref[off:off + CHUNK, :]
    xa = _silu(conv)

    row = lax.broadcasted_iota(jnp.int32, (CHUNK, 1), 0)
    dt = _softplus(dt_ref[...] + dtb_ref[...])
    dt = jnp.where(jnp.logical_or(c > 0, row >= PAD), dt, 0.0)
    dt_e = jnp.concatenate([_dot_sel_rhs(dt[:, b * LANES:(b + 1) * LANES], sel_ref[...]) for b in batches],
                           axis=1)
    d_e = dt_e * (-jnp.exp(alog_ref[...]))
    li = lax.broadcasted_iota(jnp.int32, (CHUNK, CHUNK), 0)
    si = lax.broadcasted_iota(jnp.int32, (CHUNK, CHUNK), 1)
    acum_e = _dot_sel_lhs((li >= si).astype(BF16), d_e)
    tri_t = (li <= si).astype(BF16)
    hrows = 2 * SUBLANES
    d_t = jnp.concatenate(
        [d_e[:, b * SSD_INNER + h * SSD_HEADDIM:b * SSD_INNER + h * SSD_HEADDIM + 1] if h < SSD_HEADS
         else jnp.zeros((CHUNK, 1), F32) for b in batches for h in range(hrows)], axis=1).T
    acum_t = _dot_sel_rhs(d_t, jnp.concatenate([tri_t, tri_t], axis=1))

    l2 = lax.broadcasted_iota(jnp.int32, (CHUNK, LANES), 0)
    lane = lax.broadcasted_iota(jnp.int32, (CHUNK, LANES), 1)
    low = lane < SSD_HEADDIM
    causal2 = l2 >= jnp.where(low, lane, lane - SSD_HEADDIM)

    groups = [(b, g) for b in batches for g in range(SSD_GROUPS)]
    chains = [(b, p) for b in batches for p in range(SSD_PAIRS)]
    gi = lambda b, p: b * SSD_GROUPS + p // (SSD_PAIRS // SSD_GROUPS)
    xcol = lambda b, off: slice(b * SSD_XBC + off, b * SSD_XBC + off + SSD_STATE)
    ycol = lambda b, p: slice(b * SSD_INNER + p * LANES, b * SSD_INNER + (p + 1) * LANES)
    bg = [xa[:, xcol(b, SSD_INNER + g * SSD_STATE)] for b, g in groups]
    cg = [xa[:, xcol(b, SSD_INNER + (SSD_GROUPS + g) * SSD_STATE)] for b, g in groups]
    cb2 = [_dot_nt(cg[i], jnp.concatenate([bg[i], bg[i]], axis=0)) for i in range(len(groups))]
    bg_t = [t.T for t in bg]
    xs = [xa[:, b * SSD_XBC + p * LANES:b * SSD_XBC + (p + 1) * LANES] for b, p in chains]
    ac = [acum_e[:, ycol(b, p)] for b, p in chains]
    a_last = [acum_e[CHUNK - 1:CHUNK, ycol(b, p)] for b, p in chains]
    ac_s = [jnp.where(low[0:1], acum_t[b * hrows + 2 * p:b * hrows + 2 * p + 1],
                      acum_t[b * hrows + 2 * p + 1:b * hrows + 2 * p + 2]) for b, p in chains]
    n = range(len(chains))
    m = [jnp.where(causal2, cb2[gi(*chains[i])] * jnp.exp(jnp.minimum(ac[i] - ac_s[i], 0.0)), 0.0) for i in n]
    xdt = [xs[i] * dt_e[:, ycol(*chains[i])] for i in n]
    xdt_bd = [jnp.concatenate([jnp.where(low, xdt[i], 0.0), jnp.where(low, 0.0, xdt[i])], axis=0)
              for i in n]
    st = [st_ref[i] for i in n]
    y_in = [_dot(m[i], xdt_bd[i]) for i in n]
    y_st = [_dot(cg[gi(*chains[i])], st[i]) for i in n]
    st_in = [_dot(bg_t[gi(*chains[i])], xdt[i] * jnp.exp(a_last[i] - ac[i])) for i in n]
    for i in n:
        st_ref[i] = st[i] * jnp.exp(a_last[i]) + st_in[i]
    y = jnp.concatenate([y_in[i] + y_st[i] * jnp.exp(ac[i]) for i in n], axis=1)
    xs_all = jnp.concatenate([xa[:, b * SSD_XBC:b * SSD_XBC + SSD_INNER] for b in batches], axis=1)
    y = (y + xs_all * dsk_ref[...]) * _silu(z_ref[...])
    gw = SSD_INNER // SSD_GROUPS
    for j in range(bpb * SSD_GROUPS):
        o_ref[:, j * gw:(j + 1) * gw] = _rms(y[:, j * gw:(j + 1) * gw], nw_ref[:, j * gw:(j + 1) * gw])


SSD_BPB = 4


def _ssd(z, xbc, dtr, conv_w, conv_b, dt_bias, a_log, d_skip, norm_w, nb, lp):
    bpb = SSD_BPB if nb % SSD_BPB == 0 else 1
    rep = lambda t: jnp.tile(t, (1, bpb))
    pad_h = lambda t: jnp.zeros((1, LANES), F32).at[0, :SSD_HEADS].set(t.astype(F32))
    per_ch = lambda t: jnp.repeat(t.astype(F32), SSD_HEADDIM).reshape(1, SSD_INNER)
    cw8 = jnp.zeros((SUBLANES, SSD_XBC), F32).at[:SSD_CONV].set(conv_w.astype(F32))
    spec = lambda width: pl.BlockSpec((CHUNK, bpb * width), lambda b, c: (c, b))
    out = pl.pallas_call(
        functools.partial(_ssd_kernel, bpb=bpb),
        grid=(nb // bpb, lp // CHUNK),
        in_specs=[spec(SSD_INNER), spec(SSD_XBC), spec(LANES),
                  _const_spec((SUBLANES, bpb * SSD_XBC)), _const_spec((1, bpb * SSD_XBC)),
                  _const_spec((1, bpb * LANES)), _const_spec((1, bpb * SSD_INNER)),
                  _const_spec((1, bpb * SSD_INNER)), _const_spec((1, bpb * SSD_INNER)),
                  _const_spec((LANES, SSD_INNER))],
        out_specs=spec(SSD_INNER),
        out_shape=jax.ShapeDtypeStruct((lp, nb * SSD_INNER), F32),
        scratch_shapes=[pltpu.VMEM((SUBLANES, bpb * SSD_XBC), F32),
                        pltpu.VMEM((CHUNK + SUBLANES, bpb * SSD_XBC), F32),
                        pltpu.VMEM((bpb * SSD_PAIRS, SSD_STATE, LANES), F32)],
        compiler_params=_cparams("arbitrary", "arbitrary"),
        name="ssd",
    )(z, xbc, dtr, rep(cw8), rep(conv_b.reshape(1, SSD_XBC).astype(F32)), rep(pad_h(dt_bias)),
      rep(per_ch(a_log)), rep(per_ch(d_skip)), rep(norm_w.reshape(1, SSD_INNER).astype(F32)),
      jnp.asarray(_ssd_tables(), BF16))
    return out


HG_LEVELS = 6


def _hgrn_tables():
    e = np.zeros((2 + HG_LEVELS, CHUNK, CHUNK), np.float32)
    m = np.zeros((1 + HG_LEVELS, CHUNK, CHUNK), np.float32)
    idx = np.arange(CHUNK)
    e[0] = idx[None, :] <= idx[:, None]
    e[1] = idx[None, :] > idx[:, None]
    m[0] = np.eye(CHUNK)
    for k in range(1, HG_LEVELS + 1):
        half = 1 << (k - 1)
        for r in range(CHUNK):
            bound = ((r >> k) << k) + half - 1
            if r > bound:
                e[1 + k, r, bound + 1:r + 1] = 1.0
            else:
                e[1 + k, r, r + 1:bound + 1] = 1.0
        same = (idx[:, None] >> k) == (idx[None, :] >> k)
        upper = ((idx[:, None] >> (k - 1)) & 1) == 1
        lower = ((idx[None, :] >> (k - 1)) & 1) == 0
        m[k] = same & upper & lower
    return e.reshape(-1, CHUNK), m


def _hgrn_kernel(p_ref, lb_ref, nw_ref, e_ref, m_ref, o_ref, st_ref, *, bpb):
    @pl.when(pl.program_id(1) == 0)
    def _():
        st_ref[...] = jnp.zeros_like(st_ref)

    hd, wd = HGRN_HEADDIM, HGRN_WIDTH
    chains = range(bpb * HGRN_HEADS)
    part = lambda i: jnp.concatenate([p_ref[:, b * HGRN_IN + i * wd:b * HGRN_IN + (i + 1) * wd]
                                      for b in range(bpb)], axis=1)
    tile = lambda ref: jnp.concatenate([ref[...]] * bpb, axis=1)
    cs = lambda x, c: x[:, c * hd:(c + 1) * hd]
    q = _silu(part(0))
    lb = tile(lb_ref)
    forget = lb + (1.0 - lb) * _sigmoid(part(1))
    kf = 1.0 - forget
    eg = jnp.exp(_dot_sel_lhs(e_ref[...], jnp.log(forget)))
    att = [m_ref[0] * _dot_nt(cs(q, c), cs(kf, c)) for c in chains]
    for k in range(1, HG_LEVELS + 1):
        ek = eg[(1 + k) * CHUNK:(2 + k) * CHUNK]
        qe, ke = q * ek, kf * ek
        att = [att[c] + m_ref[k] * _dot_nt(cs(qe, c), cs(ke, c)) for c in chains]
    qg = q * eg[0:CHUNK]
    kg = kf * eg[CHUNK:2 * CHUNK]
    iv = part(2)
    st = [st_ref[c] for c in chains]
    out = [_dot(att[c], cs(iv, c)) + _dot_nt(cs(qg, c), st[c]) for c in chains]
    for c in chains:
        st_ref[c] = st[c] * cs(eg[CHUNK - 1:CHUNK], c) + _dot_tn(cs(iv, c), cs(kg, c))
    og = _silu(part(3))
    nw = tile(nw_ref)
    for c in chains:
        o_ref[:, c * hd:(c + 1) * hd] = _rms(out[c], cs(nw, c)) * cs(og, c)


HGRN_BPB = 4


def _hgrn(p, lb, norm_w, nb, lp):
    bpb = HGRN_BPB if nb % HGRN_BPB == 0 else 1
    e_np, m_np = _hgrn_tables()
    spec = lambda width: pl.BlockSpec((CHUNK, bpb * width), lambda b, c: (c, b))
    out = pl.pallas_call(
        functools.partial(_hgrn_kernel, bpb=bpb),
        grid=(nb // bpb, lp // CHUNK),
        in_specs=[spec(HGRN_IN), _const_spec((1, HGRN_WIDTH)), _const_spec((1, HGRN_WIDTH)),
                  _const_spec(e_np.shape), _const_spec(m_np.shape)],
        out_specs=spec(HGRN_WIDTH),
        out_shape=jax.ShapeDtypeStruct((lp, nb * HGRN_WIDTH), F32),
        scratch_shapes=[pltpu.VMEM((bpb * HGRN_HEADS, HGRN_HEADDIM, HGRN_HEADDIM), F32)],
        compiler_params=_cparams("arbitrary", "arbitrary"),
        name="hgrn2",
    )(p, lb.reshape(1, HGRN_WIDTH).astype(F32),
      norm_w.reshape(1, HGRN_WIDTH).astype(F32), jnp.asarray(e_np, BF16), jnp.asarray(m_np, F32))
    return out


def _head_ones():
    idx = np.arange(RWKV_WIDTH) // RWKV_HEADDIM
    return (idx[:, None] == idx[None, :]).astype(np.float32)


def _rwkv_prep_kernel(*refs, tm, nb, has_vres):
    (p_ref, mu_ref, w0_ref, w2_ref, a0_ref, a2_ref, g2_ref, kk_ref, ka_ref, rk_ref, ones_ref) = refs[:11]
    if has_vres:
        vf_ref, v0_ref, v2_ref = refs[11:14]
        kkb_o, wk_o, rv_o, g_o, bv_o, carry_ref, ext_ref = refs[14:]
    else:
        kkb_o, wk_o, rv_o, g_o, bv_o, v_o, carry_ref, ext_ref = refs[11:]

    @pl.when(pl.program_id(0) == 0)
    def _():
        carry_ref[...] = jnp.zeros_like(carry_ref)

    p = p_ref[...]
    ext_ref[0:nb, :] = carry_ref[...]
    ext_ref[nb:nb + tm, :] = p
    carry_ref[...] = p[tm - nb:tm, :]
    prev = ext_ref[0:tm, :]
    ps = p + (prev - p) * mu_ref[...]

    wd = RWKV_WIDTH
    r, k, v = ps[:, :wd], ps[:, wd:2 * wd], ps[:, 2 * wd:3 * wd]
    pwa = ps[:, 3 * wd:3 * wd + LORA_W + LORA_A]
    pg = ps[:, 3 * wd + LORA_W + LORA_A:RWKV_IN]
    w_log = -_softplus(-(w0_ref[...] + _dot(jnp.tanh(pwa), w2_ref[...]))) - 0.5
    w = jnp.exp(-jnp.exp(w_log))
    a = _sigmoid(a0_ref[...] + _dot(pwa, a2_ref[...]))
    if has_vres:
        pv = ps[:, RWKV_IN:RWKV_INP]
        v = v + (vf_ref[...] - v) * _sigmoid(v0_ref[...] + _dot(pv, v2_ref[...]))
    else:
        v_o[...] = v
    g_o[...] = _dot(_sigmoid(pg), g2_ref[...])
    kk = k * kk_ref[...]
    ss = _dot_sel_rhs(kk * kk, ones_ref[...])
    kk = kk * lax.rsqrt(jnp.maximum(ss, 1e-24))
    k2 = k * (1.0 + (a - 1.0) * ka_ref[...])
    bonus = _dot_sel_rhs(r * k2 * rk_ref[...], ones_ref[...])
    bv_o[...] = bonus * v
    hd = RWKV_HEADDIM
    for o_ref, first, second in ((kkb_o, kk, kk * a), (wk_o, w, k2), (rv_o, r, v)):
        for h in range(RWKV_HEADS):
            both = jnp.concatenate([first[:, h * hd:(h + 1) * hd], second[:, h * hd:(h + 1) * hd]], axis=1)
            o_ref[:, h] = both.reshape(tm // nb, nb, 2 * hd)


def _rwkv_prep(p, mu, w0, w2, a0, a2, g2, k_k, k_a, r_k, vres, tm, nb):
    rows = p.shape[0]
    wd = RWKV_WIDTH
    row = lambda t: t.reshape(1, -1).astype(F32)
    w2p = jnp.zeros((LORA_W + LORA_A, wd), F32).at[:LORA_W].set(w2).astype(BF16)
    a2p = jnp.zeros((LORA_W + LORA_A, wd), F32).at[LORA_W:].set(a2).astype(BF16)
    args = [p, row(mu), row(w0), w2p, row(a0), a2p, g2.astype(BF16), row(k_k), row(k_a), row(r_k),
            jnp.asarray(_head_ones(), BF16)]
    specs = [pl.BlockSpec((tm, RWKV_INP), lambda i: (i, 0)), _const_spec((1, RWKV_INP)),
             _const_spec((1, wd)), _const_spec((LORA_W + LORA_A, wd)), _const_spec((1, wd)),
             _const_spec((LORA_W + LORA_A, wd)), _const_spec((LORA_G, wd)), _const_spec((1, wd)),
             _const_spec((1, wd)), _const_spec((1, wd)), _const_spec((wd, wd))]
    if vres is not None:
        v_first, v0, v2 = vres
        v2p = jnp.zeros((RWKV_INP - RWKV_IN, wd), F32).at[:LORA_V].set(v2).astype(BF16)
        args += [v_first, row(v0), v2p]
        specs += [pl.BlockSpec((tm, wd), lambda i: (i, 0)), _const_spec((1, wd)),
                  _const_spec((RWKV_INP - RWKV_IN, wd))]
    pair_shape = (rows // nb, RWKV_HEADS, nb, 2 * RWKV_HEADDIM)
    pair_spec = pl.BlockSpec((tm // nb,) + pair_shape[1:], lambda i: (i, 0, 0, 0))
    row_spec = pl.BlockSpec((tm, wd), lambda i: (i, 0))
    row_shape = jax.ShapeDtypeStruct((rows, wd), F32)
    n_row = 2 if vres is not None else 3
    return pl.pallas_call(
        functools.partial(_rwkv_prep_kernel, tm=tm, nb=nb, has_vres=vres is not None),
        grid=(rows // tm,),
        in_specs=specs,
        out_specs=[pair_spec] * 3 + [row_spec] * n_row,
        out_shape=[jax.ShapeDtypeStruct(pair_shape, F32)] * 3 + [row_shape] * n_row,
        scratch_shapes=[pltpu.VMEM((nb, RWKV_INP), F32),
                        pltpu.VMEM((tm + nb, RWKV_INP), F32)],
        compiler_params=_cparams("arbitrary"),
        name="rwkv_prep",
    )(*args)


RW_VLO = RWKV_HEADDIM // 2
RW_ACC = 4


RW_LAG = 4


def _rwkv_scan_kernel(kkb_ref, wk_ref, rv_ref, o_ref,
                      m_ref, rs_ref, ws_ref, ks_ref, kks_ref, bs_ref, vs_ref, os_ref, *, tb, ni):
    @pl.when(pl.program_id(0) == 0)
    def _():
        m_ref[...] = jnp.zeros_like(m_ref)

    hd = RWKV_HEADDIM
    nl = 2 * ni
    shape = (RW_VLO, nl)

    low_lanes = lax.broadcasted_iota(jnp.int32, shape, 1) < ni

    def load(t):
        for src, first, second in ((kkb_ref, kks_ref, bs_ref), (wk_ref, ws_ref, ks_ref), (rv_ref, rs_ref, None)):
            x = src[t].reshape(ni, 2 * hd)
            tt = jnp.concatenate([x, x], axis=0).T
            first[t] = tt[:hd]
            if second is not None:
                second[t] = tt[hd:]
            else:
                vs_ref[t] = jnp.where(low_lanes, tt[hd:hd + RW_VLO], tt[hd + RW_VLO:])

    def tree(parts):
        while len(parts) > 1:
            parts = [parts[i] + parts[i + 1] for i in range(0, len(parts), 2)]
        return parts[0]

    def step(t):
        vt = vs_ref[t]
        acc = [None] * RW_ACC
        for k in range(hd):
            term = m_ref[k] * jnp.broadcast_to(kks_ref[t, k:k + 1, :], shape)
            acc[k % RW_ACC] = term if acc[k % RW_ACC] is None else acc[k % RW_ACC] + term
        sa = tree(acc)
        acc = [None] * RW_ACC
        for k in range(hd):
            bc = lambda ref: jnp.broadcast_to(ref[t, k:k + 1, :], shape)
            mk = m_ref[k] * bc(ws_ref) - sa * bc(bs_ref) + vt * bc(ks_ref)
            m_ref[k] = mk
            term = mk * bc(rs_ref)
            acc[k % RW_ACC] = term if acc[k % RW_ACC] is None else acc[k % RW_ACC] + term
        os_ref[t] = tree(acc)

    def store(t):
        o = os_ref[t]
        o_ref[t] = jnp.concatenate([o[:, :ni], o[:, ni:]], axis=0).T.reshape(o_ref.shape[1:])

    def pair(t0, do_load, do_store):
        if do_load:
            load(t0 + RW_LAG)
            load(t0 + RW_LAG + 1)
        if do_store:
            store(t0 - 2)
            store(t0 - 1)
        step(t0)
        step(t0 + 1)

    for t in range(RW_LAG):
        load(t)
    pair(0, True, False)

    def body(i, carry):
        pair(2 * i, True, True)
        return carry

    n_pairs = tb // 2
    lax.fori_loop(1, n_pairs - RW_LAG // 2, body, 0)
    for i in range(n_pairs - RW_LAG // 2, n_pairs):
        pair(2 * i, False, True)
    store(tb - 2)
    store(tb - 1)


def _rwkv_scan(kkb, wk, rv, nb, lp, tb):
    nh, hd = RWKV_HEADS, RWKV_HEADDIM
    ni = nb * nh
    nl = 2 * ni
    in_spec = pl.BlockSpec((tb, nh, nb, 2 * hd), lambda i: (i, 0, 0, 0))
    return pl.pallas_call(
        functools.partial(_rwkv_scan_kernel, tb=tb, ni=ni),
        grid=(lp // tb,),
        in_specs=[in_spec] * 3,
        out_specs=pl.BlockSpec((tb, nh, nb, hd), lambda i: (i, 0, 0, 0)),
        out_shape=jax.ShapeDtypeStruct((lp, nh, nb, hd), F32),
        scratch_shapes=[pltpu.VMEM((hd, RW_VLO, nl), F32)] + [pltpu.VMEM((tb, hd, nl), F32)] * 5
                       + [pltpu.VMEM((tb, RW_VLO, nl), F32)] * 2,
        compiler_params=_cparams("arbitrary"),
        name="rwkv_scan",
    )(kkb, wk, rv)


def _rwkv_post_kernel(o_ref, bv_ref, g_ref, lw_ref, lb_ref, ones_ref, y_ref, *, tm):
    o = jnp.concatenate([o_ref[:, h].reshape(tm, RWKV_HEADDIM) for h in range(RWKV_HEADS)], axis=1)
    inv = 1.0 / RWKV_HEADDIM
    mean = _dot_sel_rhs(o, ones_ref[...]) * inv
    xc = o - mean
    var = _dot_sel_rhs(xc * xc, ones_ref[...]) * inv
    y = xc * lax.rsqrt(var + GN_EPS) * lw_ref[...] + lb_ref[...] + bv_ref[...]
    y_ref[...] = y * g_ref[...]


def _rwkv_post(o, bv, g, ln_w, ln_b, tm, nb):
    rows = bv.shape[0]
    wd = RWKV_WIDTH
    blk = pl.BlockSpec((tm, wd), lambda i: (i, 0))
    return pl.pallas_call(
        functools.partial(_rwkv_post_kernel, tm=tm),
        grid=(rows // tm,),
        in_specs=[pl.BlockSpec((tm // nb, RWKV_HEADS, nb, RWKV_HEADDIM), lambda i: (i, 0, 0, 0)),
                  blk, blk, _const_spec((1, wd)), _const_spec((1, wd)), _const_spec((wd, wd))],
        out_specs=blk,
        out_shape=jax.ShapeDtypeStruct((rows, wd), F32),
        compiler_params=_cparams("arbitrary"),
        name="rwkv_post",
    )(o, bv, g, ln_w.reshape(1, wd).astype(F32), ln_b.reshape(1, wd).astype(F32),
      jnp.asarray(_head_ones(), BF16))


def _even_weight(w_in):
    padw = jnp.zeros((D_MODEL, LANES - SSD_HEADS), w_in.dtype)
    return jnp.concatenate([w_in, padw], axis=1).astype(BF16)


def kernel(x, meta, norm_mix_pre, norm_mix_post, norm_ffn_pre, norm_ffn_post, mix_w_out, ffn_w_up, ffn_conv_w, ffn_conv_b, ffn_w_down, ev_w_in, s5_lam_re, s5_lam_im, s5_log_dt, s5_b_re, s5_b_im, s5_c_re, s5_c_im, s5_d, s5_w_glu, s5_b_glu, ssd_conv_w, ssd_conv_b, ssd_dt_bias, ssd_a_log, ssd_d, ssd_norm, od_w_in, rw_mu, rw_w0, rw_w2, rw_a0, rw_a2, rw_g2, rw_k_k, rw_k_a, rw_r_k, rw_ln_w, rw_ln_b, rw_w_vin, rw_mu_v, rw_v0, rw_v2, hg_lb_raw, hg_norm):
    nb, seq, _ = x.shape
    depth = norm_mix_pre.shape[0]
    lp = PAD + N_META + seq
    rows = lp * nb
    tm = _row_tile(rows, 512)
    h = jnp.concatenate([jnp.zeros((PAD, nb, D_MODEL), x.dtype),
                         jnp.broadcast_to(meta.astype(x.dtype)[:, None], (N_META, nb, D_MODEL)),
                         x.transpose(1, 0, 2)], axis=0)
    h = h.reshape(rows, D_MODEL)

    lb_w = jax.nn.softmax(hg_lb_raw.astype(F32), axis=0)
    lb_table = jnp.cumsum(lb_w, axis=0) - lb_w[0]
    v_first = None
    s1 = S5_WIDTH
    s2 = s1 + SSD_INNER
    s3 = s2 + SSD_XBC
    for layer in range(depth):
        if layer % 2 == 0:
            e = layer // 2
            u, z, xbc, dtr = _norm_proj(h, norm_mix_pre[layer], _even_weight(ev_w_in[e]),
                                        ((0, s1, False), (s1, SSD_INNER, True), (s2, SSD_XBC, True),
                                         (s3, LANES, True)), tm, nb)
            tables = _s5_tables(s5_lam_re[e], s5_lam_im[e], s5_log_dt[e], s5_b_re[e], s5_b_im[e],
                                s5_c_re[e], s5_c_im[e])
            y_a = _s5(u, tables, s5_d[e], s5_w_glu[e], s5_b_glu[e], nb, lp, CHUNK)
            y_b = _ssd(z, xbc, dtr, ssd_conv_w[e], ssd_conv_b[e], ssd_dt_bias[e], ssd_a_log[e], ssd_d[e],
                       ssd_norm[e], nb, lp)
        else:
            o = layer // 2
            mu = jnp.zeros((RWKV_INP,), F32).at[:RWKV_IN].set(rw_mu[o])
            w_rw = jnp.zeros((D_MODEL, RWKV_INP), F32).at[:, :RWKV_IN].set(od_w_in[o][:, :RWKV_IN])
            if o > 0:
                mu = mu.at[RWKV_IN:RWKV_IN + LORA_V].set(rw_mu_v[o - 1])
                w_rw = w_rw.at[:, RWKV_IN:RWKV_IN + LORA_V].set(rw_w_vin[o - 1])
            w_in = jnp.concatenate([w_rw, od_w_in[o][:, RWKV_IN:]], axis=1).astype(BF16)
            p_rw, p_hg = _norm_proj(h, norm_mix_pre[layer], w_in,
                                    ((0, RWKV_INP, False), (RWKV_INP, HGRN_IN, True)), tm, nb)
            vres = None if o == 0 else (v_first, rw_v0[o - 1], rw_v2[o - 1])
            outs = _rwkv_prep(p_rw, mu, rw_w0[o], rw_w2[o], rw_a0[o], rw_a2[o], rw_g2[o], rw_k_k[o],
                              rw_k_a[o], rw_r_k[o], vres, tm, nb)
            kkb, wk, rv, g, bv = outs[:5]
            if o == 0:
                v_first = outs[5]
            o_rw = _rwkv_scan(kkb, wk, rv, nb, lp, 64)
            y_a = _rwkv_post(o_rw, bv, g, rw_ln_w[o], rw_ln_b[o], tm, nb)
            y_b = _hgrn(p_hg, lb_table[o], hg_norm[o], nb, lp)
        ka = y_a.shape[1]
        w_out = mix_w_out[layer].astype(BF16)
        h = _out_proj(y_a, y_b, w_out[:ka], w_out[ka:], norm_mix_post[layer], h, tm, nb)
        h = _ffn(h, norm_ffn_pre[layer], ffn_w_up[layer].astype(BF16), ffn_conv_w[layer], ffn_conv_b[layer],
                 ffn_w_down[layer].astype(BF16), norm_ffn_post[layer], tm, nb)
    return h.reshape(lp, nb, D_MODEL)[PAD + N_META:].transpose(1, 0, 2)
```
